```python
import math
import jax
import jax.numpy as jnp
from jax import lax
import numpy as np

D_MODEL = 1024
BATCH = 32
SEQ = 2048
DEPTH = 4

GRID_W = 64
CTX_LEN = 256
Q_BLOCK = 128
ROPE_THETA = 10000.0
NORM_EPS = 1e-6
FFN_RES = 0.5
N_BRANCH = 3
N_MOD = 9

MLA_HEADS = 16
MLA_NOPE = 64
MLA_ROPE = 32
MLA_V = 64
MLA_Q_RANK = 512
MLA_KV_RANK = 256
GQA_HEADS = 16
GQA_KV_HEADS = 4
GQA_REP = GQA_HEADS // GQA_KV_HEADS
GQA_HD = 64
SSD_INNER = 1024
SSD_HD = 64
SSD_HEADS = SSD_INNER // SSD_HD
SSD_GROUPS = 4
SSD_HPG = SSD_HEADS // SSD_GROUPS
SSD_STATE = 128
SSD_CONV = 5
SSD_CHUNK = 128
SSD_CONV_DIM = SSD_INNER + 2 * SSD_GROUPS * SSD_STATE
D_FF = 2816

MEM_SIZES = (MLA_KV_RANK + MLA_ROPE, GQA_KV_HEADS * GQA_HD, GQA_KV_HEADS * GQA_HD, SSD_CONV_DIM, 2 * SSD_HEADS)
QRY_SIZES = (MLA_Q_RANK, GQA_HEADS * GQA_HD, SSD_INNER, N_BRANCH * D_MODEL)
N_MEM_COLS = MLA_KV_RANK + MLA_ROPE + 2 * GQA_KV_HEADS * GQA_HD + SSD_CONV_DIM + 2 * SSD_HEADS
N_IN_COLS = N_MEM_COLS + MLA_Q_RANK + GQA_HEADS * GQA_HD + SSD_INNER + N_BRANCH * D_MODEL
MLA_SCALE = (MLA_NOPE + MLA_ROPE) ** -0.5
GQA_SCALE = GQA_HD ** -0.5

kernel_name = 'hybrid_mla_gqa_ssd_macaron_dit'


def rms_norm(t, g):
    tf = t.astype(jnp.float32)
    y = tf * lax.rsqrt(jnp.mean(tf * tf, axis=-1, keepdims=True) + NORM_EPS)
    return (y * g.astype(jnp.float32)).astype(t.dtype)


def split_cols(h, sizes):
    parts, start = [], 0
    for s in sizes:
        parts.append(h[..., start:start + s])
        start += s
    return parts


def axial_rope(rows, rot_dim):
    n_freq = rot_dim // 4
    row = jnp.repeat(jnp.arange(rows, dtype=jnp.float32), GRID_W)
    col = jnp.tile(jnp.arange(GRID_W, dtype=jnp.float32), rows)
    inv = ROPE_THETA ** (-jnp.arange(n_freq, dtype=jnp.float32) / n_freq)
    ang = jnp.concatenate([row[:, None] * inv, col[:, None] * inv], axis=-1)
    return jnp.cos(ang), jnp.sin(ang)


def apply_rope(t, cos, sin):
    tf = t.astype(jnp.float32)
    half = t.shape[-1] // 2
    t1, t2 = tf[..., :half], tf[..., half:]
    cs, sn = cos[None, :, None, :], sin[None, :, None, :]
    return jnp.concatenate([t1 * cs - t2 * sn, t2 * cs + t1 * sn], axis=-1).astype(t.dtype)


def blocked_attention(q, k, v, scale):
    b, lq, kh, r, d = q.shape
    qb = q.reshape(b, lq // Q_BLOCK, Q_BLOCK, kh, r, d).swapaxes(0, 1)

    def one_block(qblk):
        s = jnp.einsum('bqkrd,bskd->bkrqs', qblk, k).astype(jnp.float32) * scale
        p = jax.nn.softmax(s, axis=-1).astype(v.dtype)
        return jnp.einsum('bkrqs,bskv->bqkrv', p, v)

    o = lax.map(one_block, qb)
    return o.swapaxes(0, 1).reshape(b, lq, kh, r, v.shape[-1])


def centred_depthwise_conv(t, w, bias):
    pad = SSD_CONV // 2
    out = lax.conv_general_dilated(t, w[:, None, :].astype(t.dtype), window_strides=(1,), padding=[(pad, pad)], dimension_numbers=('NWC', 'WIO', 'NWC'), feature_group_count=t.shape[-1])
    return out + bias.astype(t.dtype)


def swiglu(u, wg, wu, wd):
    return (jax.nn.silu(u @ wg) * (u @ wu)) @ wd


def ffn_half_step(h, shift, scale, gate, g_in, g_outn, wg, wu, wd):
    u = rms_norm(h, g_in) * (1 + scale) + shift
    return h + FFN_RES * gate * rms_norm(swiglu(u, wg, wu, wd), g_outn)


def ssd_prepare(xs, dt, a, bm, h0):
    b, n = xs.shape[:2]
    nc = n // SSD_CHUNK
    xd = (xs * dt[..., None]).reshape(b, nc, SSD_CHUNK, SSD_GROUPS, SSD_HPG, SSD_HD)
    a_cs = jnp.cumsum((dt * a).reshape(b, nc, SSD_CHUNK, SSD_GROUPS, SSD_HPG), axis=2)
    bc = bm.reshape(b, nc, SSD_CHUNK, SSD_GROUPS, SSD_STATE)
    decay_to_end = jnp.exp(a_cs[:, :, -1:] - a_cs)
    chunk_states = jnp.einsum('bcqgn,bcqgj,bcqgjp->bcgjpn', bc, decay_to_end, xd)
    chunk_decay = jnp.exp(a_cs[:, :, -1])

    def step(h, inp):
        dec, st = inp
        return dec[..., None, None] * h + st, h

    final, h_enter = lax.scan(step, h0, (jnp.moveaxis(chunk_decay, 1, 0), jnp.moveaxis(chunk_states, 1, 0)))
    return (xd, a_cs, bc, jnp.moveaxis(h_enter, 0, 1)), final


def ssd_output(xd, a_cs, bc, h_enter, cm):
    b, nc = xd.shape[:2]
    cc = cm.reshape(b, nc, SSD_CHUNK, SSD_GROUPS, SSD_STATE)
    cb = jnp.einsum('bcign,bcsgn->bcgis', cc, bc)
    seg = a_cs[:, :, :, None] - a_cs[:, :, None, :]
    order = jnp.tril(jnp.ones((SSD_CHUNK, SSD_CHUNK), dtype=bool))[:, :, None, None]
    lmat = jnp.exp(jnp.where(order, seg, -jnp.inf))
    y_diag = jnp.einsum('bcgis,bcisgj,bcsgjp->bcigjp', cb, lmat, xd)
    y_off = jnp.einsum('bcign,bcgjpn,bcigj->bcigjp', cc, h_enter, jnp.exp(a_cs))
    return (y_diag + y_off).reshape(b, nc * SSD_CHUNK, SSD_HEADS, SSD_HD)


def ssd_scans(mem, a, h0_f, h0_b):
    dt_f, dt_b = mem['dt'][..., :SSD_HEADS], mem['dt'][..., SSD_HEADS:]
    parts_f, fin_f = ssd_prepare(mem['xs'], dt_f, a[0], mem['bm'], h0_f)
    parts_b, fin_b = ssd_prepare(jnp.flip(mem['xs'], 1), jnp.flip(dt_b, 1), a[1], jnp.flip(mem['bm'], 1), h0_b)
    return parts_f, fin_f, parts_b, fin_b


def stream_memory(hm, lp, rope_mla, rope_gqa):
    b, n = hm.shape[:2]
    kv_lat, k_g, v_g, xbc, dt_raw = split_cols(hm, MEM_SIZES)
    c_kv = rms_norm(kv_lat[..., :MLA_KV_RANK], lp['g_mla_kv'])
    k_rope = kv_lat[..., MLA_KV_RANK:][:, :, None, :]
    kv = (c_kv @ lp['w_mla_kv_up']).reshape(b, n, MLA_HEADS, MLA_NOPE + MLA_V)
    k_g = rms_norm(k_g.reshape(b, n, GQA_KV_HEADS, GQA_HD), lp['g_gqa_k'])
    if rope_mla is not None:
        k_rope = apply_rope(k_rope, *rope_mla)
        k_g = apply_rope(k_g, *rope_gqa)
    k_mla = jnp.concatenate([kv[..., :MLA_NOPE], jnp.broadcast_to(k_rope, (b, n, MLA_HEADS, MLA_ROPE))], axis=-1)
    xbc = jax.nn.silu(centred_depthwise_conv(xbc, lp['conv_w'], lp['conv_b'])).astype(jnp.float32)
    gn = SSD_GROUPS * SSD_STATE
    dt = jax.nn.softplus(dt_raw.astype(jnp.float32) + lp['dt_bias'].reshape(2 * SSD_HEADS).astype(jnp.float32))
    return {
        'k_mla': k_mla,
        'v_mla': kv[..., MLA_NOPE:],
        'k_gqa': k_g,
        'v_gqa': v_g.reshape(b, n, GQA_KV_HEADS, GQA_HD),
        'xs': xbc[..., :SSD_INNER].reshape(b, n, SSD_HEADS, SSD_HD),
        'bm': xbc[..., SSD_INNER:SSD_INNER + gn].reshape(b, n, SSD_GROUPS, SSD_STATE),
        'cm': xbc[..., SSD_INNER + gn:].reshape(b, n, SSD_GROUPS, SSD_STATE),
        'dt': dt,
    }


def mixer_out(hq, mem, ctx_mem, parts_f, parts_b, lp, rope_mla, rope_gqa):
    b, n = hq.shape[:2]
    q_lat, q_g, z, gate_cols = split_cols(hq, QRY_SIZES)
    cq = rms_norm(q_lat, lp['g_mla_q'])
    q = (cq @ lp['w_mla_q_up']).reshape(b, n, MLA_HEADS, MLA_NOPE + MLA_ROPE)
    q_nope, q_rope = q[..., :MLA_NOPE], q[..., MLA_NOPE:]
    q_g = rms_norm(q_g.reshape(b, n, GQA_HEADS, GQA_HD), lp['g_gqa_q'])
    if rope_mla is not None:
        q_rope = apply_rope(q_rope, *rope_mla)
        q_g = apply_rope(q_g, *rope_gqa)
    q_mla = jnp.concatenate([q_nope, q_rope], axis=-1)[:, :, :, None, :]
    q_g = q_g.reshape(b, n, GQA_KV_HEADS, GQA_REP, GQA_HD)
    if ctx_mem is None:
        k_mla, v_mla, k_g, v_g = mem['k_mla'], mem['v_mla'], mem['k_gqa'], mem['v_gqa']
    else:
        k_mla = jnp.concatenate([mem['k_mla'], ctx_mem['k_mla']], axis=1)
        v_mla = jnp.concatenate([mem['v_mla'], ctx_mem['v_mla']], axis=1)
        k_g = jnp.concatenate([mem['k_gqa'], ctx_mem['k_gqa']], axis=1)
        v_g = jnp.concatenate([mem['v_gqa'], ctx_mem['v_gqa']], axis=1)
    o_mla = blocked_attention(q_mla, k_mla, v_mla, MLA_SCALE).reshape(b, n, MLA_HEADS * MLA_V)
    o_gqa = blocked_attention(q_g, k_g, v_g, GQA_SCALE).reshape(b, n, GQA_HEADS * GQA_HD)
    skip = jnp.sum(lp['ssd_skip'].astype(jnp.float32), axis=0)
    y_f = ssd_output(*parts_f, mem['cm'])
    y_b = jnp.flip(ssd_output(*parts_b, jnp.flip(mem['cm'], 1)), 1)
    y = (y_f + y_b + skip[:, None] * mem['xs']).reshape(b, n, SSD_INNER)
    y = rms_norm(y * jax.nn.silu(z.astype(jnp.float32)), lp['g_ssd']).astype(hq.dtype)
    g = jax.nn.sigmoid(gate_cols.astype(jnp.float32)).astype(hq.dtype).reshape(b, n, N_BRANCH, D_MODEL)
    merged = g[:, :, 0] * (o_mla @ lp['w_mla_o']) + g[:, :, 1] * (o_gqa @ lp['w_gqa_o']) + g[:, :, 2] * (y @ lp['w_ssd_o'])
    return merged @ lp['w_out']


def setup_inputs(seed: int = 0) -> dict:
    key = jax.random.key(seed)
    keys = jax.random.split(key, 32)
    counter = iter(range(32))
    f32 = jnp.float32

    def nrm(shape, scale):
        return jax.random.normal(keys[next(counter)], shape, f32) * scale

    def gain(shape):
        return 1.0 + 0.05 * jax.random.normal(keys[next(counter)], shape, f32)

    L = DEPTH
    x = nrm((BATCH, SEQ, D_MODEL), 1.0)
    c = nrm((BATCH, D_MODEL), 1.0)
    ctx = nrm((BATCH, CTX_LEN, D_MODEL), 1.0)
    c_ctx = nrm((D_MODEL,), 1.0)
    w_mod = nrm((L, D_MODEL, N_MOD * D_MODEL), 0.5 * D_MODEL ** -0.5)
    b_mod = nrm((L, N_MOD * D_MODEL), 0.02)
    g_pre = gain((L, 3, D_MODEL))
    g_post = gain((L, 3, D_MODEL))
    w_ffn_gate = nrm((L, 2, D_MODEL, D_FF), D_MODEL ** -0.5)
    w_ffn_up = nrm((L, 2, D_MODEL, D_FF), D_MODEL ** -0.5)
    w_ffn_down = nrm((L, 2, D_FF, D_MODEL), D_FF ** -0.5)
    w_in = nrm((L, D_MODEL, N_IN_COLS), D_MODEL ** -0.5)
    g_mla_q = gain((L, MLA_Q_RANK))
    w_mla_q_up = nrm((L, MLA_Q_RANK, MLA_HEADS * (MLA_NOPE + MLA_ROPE)), MLA_Q_RANK ** -0.5)
    g_mla_kv = gain((L, MLA_KV_RANK))
    w_mla_kv_up = nrm((L, MLA_KV_RANK, MLA_HEADS * (MLA_NOPE + MLA_V)), MLA_KV_RANK ** -0.5)
    g_gqa_q = gain((L, GQA_HD))
    g_gqa_k = gain((L, GQA_HD))
    conv_w = nrm((L, SSD_CONV, SSD_CONV_DIM), SSD_CONV ** -0.5)
    conv_b = nrm((L, SSD_CONV_DIM), 0.02)
    dt0 = jnp.exp(jax.random.uniform(keys[next(counter)], (L, 2, SSD_HEADS), f32, math.log(1e-3), math.log(1e-1)))
    dt_bias = dt0 + jnp.log(-jnp.expm1(-dt0))
    a_log = jnp.log(jax.random.uniform(keys[next(counter)], (L, 2, SSD_HEADS), f32, 1.0, 16.0))
    ssd_skip = gain((L, 2, SSD_HEADS))
    g_ssd = gain((L, SSD_INNER))
    w_mla_o = nrm((L, MLA_HEADS * MLA_V, D_MODEL), (MLA_HEADS * MLA_V) ** -0.5)
    w_gqa_o = nrm((L, GQA_HEADS * GQA_HD, D_MODEL), (GQA_HEADS * GQA_HD) ** -0.5)
    w_ssd_o = nrm((L, SSD_INNER, D_MODEL), SSD_INNER ** -0.5)
    w_out = nrm((L, D_MODEL, D_MODEL), D_MODEL ** -0.5)
    return {
        'x': x, 'c': c, 'ctx': ctx, 'c_ctx': c_ctx,
        'w_mod': w_mod, 'b_mod': b_mod, 'g_pre': g_pre, 'g_post': g_post,
        'w_ffn_gate': w_ffn_gate, 'w_ffn_up': w_ffn_up, 'w_ffn_down': w_ffn_down,
        'w_in': w_in, 'g_mla_q': g_mla_q, 'w_mla_q_up': w_mla_q_up,
        'g_mla_kv': g_mla_kv, 'w_mla_kv_up': w_mla_kv_up,
        'g_gqa_q': g_gqa_q, 'g_gqa_k': g_gqa_k,
        'conv_w': conv_w, 'conv_b': conv_b, 'dt_bias': dt_bias, 'a_log': a_log,
        'ssd_skip': ssd_skip, 'g_ssd': g_ssd,
        'w_mla_o': w_mla_o, 'w_gqa_o': w_gqa_o, 'w_ssd_o': w_ssd_o, 'w_out': w_out,
    }


def reference(x, c, ctx, c_ctx, w_mod, b_mod, g_pre, g_post, w_ffn_gate, w_ffn_up, w_ffn_down, w_in, g_mla_q, w_mla_q_up, g_mla_kv, w_mla_kv_up, g_gqa_q, g_gqa_k, conv_w, conv_b, dt_bias, a_log, ssd_skip, g_ssd, w_mla_o, w_gqa_o, w_ssd_o, w_out):
    b, n, _ = x.shape
    rows = n // GRID_W
    rope_mla = axial_rope(rows, MLA_ROPE)
    rope_gqa = axial_rope(rows, GQA_HD)
    h0 = jnp.zeros((b, SSD_GROUPS, SSD_HPG, SSD_HD, SSD_STATE), jnp.float32)
    cx = ctx
    for l in range(DEPTH):
        last = l == DEPTH - 1
        lp = {
            'g_mla_q': g_mla_q[l], 'w_mla_q_up': w_mla_q_up[l],
            'g_mla_kv': g_mla_kv[l], 'w_mla_kv_up': w_mla_kv_up[l],
            'g_gqa_q': g_gqa_q[l], 'g_gqa_k': g_gqa_k[l],
            'conv_w': conv_w[l], 'conv_b': conv_b[l], 'dt_bias': dt_bias[l],
            'ssd_skip': ssd_skip[l], 'g_ssd': g_ssd[l],
            'w_mla_o': w_mla_o[l], 'w_gqa_o': w_gqa_o[l], 'w_ssd_o': w_ssd_o[l], 'w_out': w_out[l],
        }
        mod_x = (jax.nn.silu(c) @ w_mod[l] + b_mod[l]).reshape(b, N_MOD, 1, D_MODEL)
        mod_c = (jax.nn.silu(c_ctx) @ w_mod[l] + b_mod[l]).reshape(N_MOD, 1, D_MODEL)

        x = ffn_half_step(x, mod_x[:, 0], mod_x[:, 1], mod_x[:, 2], g_pre[l, 0], g_post[l, 0], w_ffn_gate[l, 0], w_ffn_up[l, 0], w_ffn_down[l, 0])
        cx = ffn_half_step(cx, mod_c[0], mod_c[1], mod_c[2], g_pre[l, 0], g_post[l, 0], w_ffn_gate[l, 0], w_ffn_up[l, 0], w_ffn_down[l, 0])

        u_x = rms_norm(x, g_pre[l, 1]) * (1 + mod_x[:, 4]) + mod_x[:, 3]
        u_c = rms_norm(cx, g_pre[l, 1]) * (1 + mod_c[4]) + mod_c[3]
        h_x = u_x @ w_in[l]
        h_c = u_c @ (w_in[l][:, :N_MEM_COLS] if last else w_in[l])
        mem_x = stream_memory(h_x[..., :N_MEM_COLS], lp, rope_mla, rope_gqa)
        mem_c = stream_memory(h_c[..., :N_MEM_COLS], lp, None, None)
        a = -jnp.exp(a_log[l].astype(jnp.float32))
        parts_cf, fin_cf, parts_cb, fin_cb = ssd_scans(mem_c, a, h0, h0)
        parts_xf, _, parts_xb, _ = ssd_scans(mem_x, a, fin_cf, fin_cb)
        y_x = mixer_out(h_x[..., N_MEM_COLS:], mem_x, mem_c, parts_xf, parts_xb, lp, rope_mla, rope_gqa)
        x = x + mod_x[:, 5] * rms_norm(y_x, g_post[l, 1])
        if not last:
            y_c = mixer_out(h_c[..., N_MEM_COLS:], mem_c, None, parts_cf, parts_cb, lp, None, None)
            cx = cx + mod_c[5] * rms_norm(y_c, g_post[l, 1])
            cx = ffn_half_step(cx, mod_c[6], mod_c[7], mod_c[8], g_pre[l, 2], g_post[l, 2], w_ffn_gate[l, 1], w_ffn_up[l, 1], w_ffn_down[l, 1])

        x = ffn_half_step(x, mod_x[:, 6], mod_x[:, 7], mod_x[:, 8], g_pre[l, 2], g_post[l, 2], w_ffn_gate[l, 1], w_ffn_up[l, 1], w_ffn_down[l, 1])
    return x
```

```python
import functools
import math

import jax
import jax.numpy as jnp
from jax import lax
from jax.experimental import pallas as pl
from jax.experimental.pallas import tpu as pltpu

F32 = jnp.float32
BF16 = jnp.bfloat16

GRID_W = 64
ROPE_THETA = 10000.0
NORM_EPS = 1e-6
FFN_RES = 0.5
N_MOD = 9

MLA_HEADS = 16
MLA_NOPE = 64
MLA_ROPE = 32
MLA_V = 64
MLA_Q_RANK = 512
MLA_KV_RANK = 256
GQA_HEADS = 16
GQA_KV_HEADS = 4
GQA_HD = 64
SSD_INNER = 1024
SSD_HD = 64
SSD_HEADS = 16
SSD_GROUPS = 4
SSD_HPG = 4
SSD_STATE = 128
SSD_CONV = 5
SSD_CHUNK = 128
SSD_CONV_DIM = SSD_INNER + 2 * SSD_GROUPS * SSD_STATE
MLA_SCALE = (MLA_NOPE + MLA_ROPE) ** -0.5
GQA_SCALE = GQA_HD ** -0.5
LOG2E = math.log2(math.e)

LANE = 128
HEAD_PAD = 128
TOKEN_TILE = 256
VMEM_LIMIT = 56 * 1024 * 1024

_C_KVLAT = 0
_C_KROPE = _C_KVLAT + MLA_KV_RANK
_C_KG = _C_KROPE + MLA_ROPE
_C_VG = _C_KG + GQA_KV_HEADS * GQA_HD
_C_XBC = _C_VG + GQA_KV_HEADS * GQA_HD
_C_DT = _C_XBC + SSD_CONV_DIM
_C_QLAT = _C_DT + 2 * SSD_HEADS
_C_QG = _C_QLAT + MLA_Q_RANK
_C_Z = _C_QG + GQA_HEADS * GQA_HD
_C_GATE = _C_Z + SSD_INNER

_A_KVLAT = 0
_A_KDUP = _A_KVLAT + MLA_KV_RANK
_A_XBC = _A_KDUP + 2 * GQA_KV_HEADS * GQA_HD
_A_QLAT = _A_XBC + SSD_CONV_DIM
_A_QG = _A_QLAT + MLA_Q_RANK
_A_KR = _A_QG + GQA_HEADS * GQA_HD
_A_DT = _A_KR + HEAD_PAD
_A_END = _A_DT + LANE


def _cparams(n_axes):
    return pltpu.CompilerParams(
        dimension_semantics=("parallel",) * n_axes, vmem_limit_bytes=VMEM_LIMIT)


def _resident(shape):
    nd = len(shape)
    return pl.BlockSpec(shape, lambda *_: (0,) * nd, pipeline_mode=pl.Buffered(1))


def _rms(t, g):
    return t * lax.rsqrt(jnp.mean(t * t, axis=-1, keepdims=True) + NORM_EPS) * g


def _dot(a, b):
    return jnp.dot(a, b, preferred_element_type=F32)


def _dot_nt(a, b):
    return lax.dot_general(a, b, (((1,), (1,)), ((), ())), preferred_element_type=F32)


def _mod_rows(tile_idx, tm, n_lat, modx_ref, modc_ref, k):
    rows = tile_idx * tm + lax.broadcasted_iota(jnp.int32, (tm, 1), 0)
    return jnp.where(rows >= n_lat, modc_ref[k:k + 1, :], modx_ref[0, k:k + 1, :])


def _mod_kernel(c_ref, w_ref, b_ref, o_ref):
    s = c_ref[...]
    s = s * jax.nn.sigmoid(s)
    o_ref[0] = jnp.dot(s, w_ref[0], preferred_element_type=F32,
                       precision=lax.Precision.HIGHEST) + b_ref[0]


def _modulation(c_all, w_mod, b_mod):
    n_layers, d, _ = w_mod.shape
    rows = c_all.shape[0]
    return pl.pallas_call(
        _mod_kernel,
        grid=(n_layers, N_MOD),
        in_specs=[
            pl.BlockSpec((rows, d), lambda l, j: (0, 0)),
            pl.BlockSpec((1, d, d), lambda l, j: (l, 0, j)),
            pl.BlockSpec((1, 1, d), lambda l, j: (l, 0, j)),
        ],
        out_specs=pl.BlockSpec((1, rows, d), lambda l, j: (l, 0, j)),
        out_shape=jax.ShapeDtypeStruct((n_layers, rows, N_MOD * d), F32),
        compiler_params=_cparams(2),
        name="modulation",
    )(c_all, w_mod, b_mod.reshape(n_layers, 1, N_MOD * d))


def _ffn_kernel(x_ref, modx_ref, modc_ref, gin_ref, gout_ref, wg_ref, wu_ref, wd_ref, o_ref,
                *, tm, n_lat, k0):
    t = pl.program_id(1)
    x = x_ref[0]
    shift = _mod_rows(t, tm, n_lat, modx_ref, modc_ref, k0)
    scale = _mod_rows(t, tm, n_lat, modx_ref, modc_ref, k0 + 1)
    gate = _mod_rows(t, tm, n_lat, modx_ref, modc_ref, k0 + 2)
    u = (_rms(x, gin_ref[...]) * (1.0 + scale) + shift).astype(BF16)
    hg = _dot(u, wg_ref[...])
    hu = _dot(u, wu_ref[...])
    a = (hg * jax.nn.sigmoid(hg) * hu).astype(BF16)
    y = _dot(a, wd_ref[...])
    o_ref[0] = x + (FFN_RES * gate) * _rms(y, gout_ref[...])


def _ffn(xc, modx, modc, g_in, g_out, wg, wu, wd, *, k0, n_lat, n_rows, alias):
    b, t_all, d = xc.shape
    f = wg.shape[1]
    tm = TOKEN_TILE
    out_rows = t_all if alias else n_rows
    return pl.pallas_call(
        functools.partial(_ffn_kernel, tm=tm, n_lat=n_lat, k0=k0),
        grid=(b, n_rows // tm),
        in_specs=[
            pl.BlockSpec((1, tm, d), lambda i, j: (i, j, 0)),
            pl.BlockSpec((1, N_MOD, d), lambda i, j: (i, 0, 0)),
            _resident((N_MOD, d)),
            _resident((1, d)),
            _resident((1, d)),
            _resident((d, f)),
            _resident((d, f)),
            _resident((f, d)),
        ],
        out_specs=pl.BlockSpec((1, tm, d), lambda i, j: (i, j, 0)),
        out_shape=jax.ShapeDtypeStruct((b, out_rows, d), F32),
        input_output_aliases={0: 0} if alias else {},
        compiler_params=_cparams(2),
        name="ffn_half_step",
    )(xc, modx, modc, g_in, g_out, wg, wu, wd)


def _tile_lanes(t, reps):
    return t if reps == 1 else jnp.concatenate([t] * reps, axis=1)


def _rope_gqa(x, cos_t, sin_t):
    w = x.shape[1]
    reps = w // LANE
    lane = lax.broadcasted_iota(jnp.int32, x.shape, 1) % GQA_HD
    half = GQA_HD // 2
    partner = jnp.where(lane < half, pltpu.roll(x, w - half, 1), pltpu.roll(x, half, 1))
    return x * _tile_lanes(cos_t, reps) + partner * _tile_lanes(sin_t, reps)


def _rope_mla(x, cos_t, sin_t):
    w = x.shape[1]
    reps = w // LANE
    lane = lax.broadcasted_iota(jnp.int32, x.shape, 1) % HEAD_PAD
    half = MLA_ROPE // 2
    first = jnp.logical_and(lane >= MLA_NOPE, lane < MLA_NOPE + half)
    partner = jnp.where(first, pltpu.roll(x, w - half, 1), pltpu.roll(x, half, 1))
    return x * _tile_lanes(cos_t, reps) + partner * _tile_lanes(sin_t, reps)


def _head_norm(x, e, e_t, gain):
    ms = _dot((x * x).astype(BF16), e)
    r = lax.rsqrt(ms + NORM_EPS)
    r_hi = r.astype(BF16)
    r_lo = (r - r_hi.astype(F32)).astype(BF16)
    return x * (_dot(r_hi, e_t) + _dot(r_lo, e_t)) * gain


def _in_proj_kernel(x_ref, modx_ref, modc_ref, gpre_ref, wa_ref, wvg_t_ref, gkv_ref, wk_ref, wv_t_ref,
                    pk_ref, gq_ref, wq_ref, e_ref, e_t_ref, ggk_ref, ggq_ref,
                    mla_cos_ref, mla_sin_ref, gqa_cos_ref, gqa_sin_ref,
                    kmla_ref, vtmla_ref, qmla_ref, kgqa_ref, vtgqa_ref, qgqa_ref, xbc_ref, dt_ref,
                    *, tm, n_lat):
    t = pl.program_id(1)
    x = x_ref[0]
    shift = _mod_rows(t, tm, n_lat, modx_ref, modc_ref, 3)
    scale = _mod_rows(t, tm, n_lat, modx_ref, modc_ref, 4)
    u = (_rms(x, gpre_ref[...]) * (1.0 + scale) + shift).astype(BF16)
    h = _dot(u, wa_ref[...])
    xbc_ref[0] = h[:, _A_XBC:_A_QLAT]
    dt_ref[0] = h[:, _A_DT:_A_END]

    mla_cos = mla_cos_ref[...]
    mla_sin = mla_sin_ref[...]
    gqa_cos = gqa_cos_ref[...]
    gqa_sin = gqa_sin_ref[...]

    ckv = _rms(h[:, _A_KVLAT:_A_KDUP], gkv_ref[...]).astype(BF16)
    k_rope = _rope_mla(h[:, _A_KR:_A_DT], mla_cos, mla_sin).astype(BF16)
    kmla_ref[0] = (_dot(ckv, wk_ref[...]) + _dot(k_rope, pk_ref[...])).astype(BF16)
    vtmla_ref[0] = _dot_nt(wv_t_ref[...], ckv).astype(BF16)

    cq = _rms(h[:, _A_QLAT:_A_QG], gq_ref[...]).astype(BF16)
    q = _rope_mla(_dot(cq, wq_ref[...]), mla_cos, mla_sin)
    qmla_ref[0] = (q * (MLA_SCALE * LOG2E)).astype(BF16)

    n_k = 2 * GQA_KV_HEADS * GQA_HD
    kd = _head_norm(h[:, _A_KDUP:_A_XBC], e_ref[0:n_k, :], e_t_ref[:, 0:n_k], ggk_ref[...])
    kgqa_ref[0] = _rope_gqa(kd, gqa_cos, gqa_sin).astype(BF16)
    qg = _head_norm(h[:, _A_QG:_A_KR], e_ref[...], e_t_ref[...], ggq_ref[...])
    qgqa_ref[0] = (_rope_gqa(qg, gqa_cos, gqa_sin) * (GQA_SCALE * LOG2E)).astype(BF16)
    vtgqa_ref[0] = _dot_nt(wvg_t_ref[...], u).astype(BF16)


def _in_proj(xc, modx, modc, lw, tabs, *, n_lat):
    b, t_all, d = xc.shape
    tm = TOKEN_TILE
    n_qm = MLA_HEADS * HEAD_PAD
    n_kd = 2 * GQA_KV_HEADS * GQA_HD
    n_qg = GQA_HEADS * GQA_HD
    n_vm = MLA_HEADS * MLA_V
    n_vg = GQA_KV_HEADS * GQA_HD
    row = lambda w: pl.BlockSpec((1, tm, w), lambda i, j: (i, j, 0))
    col = lambda h: pl.BlockSpec((1, h, tm), lambda i, j: (i, 0, j))
    tab = pl.BlockSpec((tm, LANE), lambda i, j: (j, 0))
    res = [lw["g_pre1"], lw["wa"], lw["wvg_t"], lw["g_mla_kv"], lw["wk"], lw["wv_t"], lw["pk"],
           lw["g_mla_q"], lw["wq"], lw["e"], lw["e_t"], lw["g_gqa_k"], lw["g_gqa_q"]]
    return pl.pallas_call(
        functools.partial(_in_proj_kernel, tm=tm, n_lat=n_lat),
        grid=(b, t_all // tm),
        in_specs=[row(d), pl.BlockSpec((1, N_MOD, d), lambda i, j: (i, 0, 0)), _resident((N_MOD, d))]
        + [_resident(w.shape) for w in res] + [tab] * 4,
        out_specs=[row(n_qm), col(n_vm), row(n_qm), row(n_kd), col(n_vg), row(n_qg),
                   row(SSD_CONV_DIM), row(LANE)],
        out_shape=[
            jax.ShapeDtypeStruct((b, t_all, n_qm), BF16),
            jax.ShapeDtypeStruct((b, n_vm, t_all), BF16),
            jax.ShapeDtypeStruct((b, t_all, n_qm), BF16),
            jax.ShapeDtypeStruct((b, t_all, n_kd), BF16),
            jax.ShapeDtypeStruct((b, n_vg, t_all), BF16),
            jax.ShapeDtypeStruct((b, t_all, n_qg), BF16),
            jax.ShapeDtypeStruct((b, t_all, SSD_CONV_DIM), F32),
            jax.ShapeDtypeStruct((b, t_all, LANE), F32),
        ],
        compiler_params=_cparams(2),
        name="in_proj",
    )(xc, modx, modc, *res, *tabs)


_SSD_W = SSD_HPG * SSD_HD + 2 * SSD_STATE
_SSD_B0 = SSD_HPG * SSD_HD
_SSD_C0 = _SSD_B0 + SSD_STATE
_PAD_ROWS = 8


def _softplus(v):
    return jnp.maximum(v, 0.0) + jnp.log1p(jnp.exp(-jnp.abs(v)))


def _ssd_kernel(xs_ref, bm_ref, cm_ref, dt_ref, wx_ref, wb_ref, wc_ref, bx_ref, bb_ref, bc_ref,
                par_ref, skip_ref, y_ref, pad_ref, act_ref, dtv_ref, av_ref, h_ref, *, n_lat, n_ctx):
    cl = SSD_CHUNK
    t_all = n_lat + n_ctx
    n_chunks = t_all // cl
    n_ctx_chunks = n_ctx // cl
    n_lat_chunks = n_lat // cl
    g = pl.program_id(1)

    def pad_row(r):
        return pl.multiple_of(r + jnp.where(r >= n_lat, 2 * _PAD_ROWS, _PAD_ROWS), _PAD_ROWS)

    zeros = jnp.zeros((_PAD_ROWS, _SSD_W), F32)
    pad_ref[0:_PAD_ROWS, :] = zeros
    pad_ref[_PAD_ROWS + n_lat:2 * _PAD_ROWS + n_lat, :] = zeros
    pad_ref[2 * _PAD_ROWS + t_all:3 * _PAD_ROWS + t_all, :] = zeros

    def copy_body(i, carry):
        r = pl.multiple_of(i * cl, cl)
        dst = pad_row(r)
        pad_ref[pl.ds(dst, cl), 0:_SSD_B0] = xs_ref[0, pl.ds(r, cl), :]
        pad_ref[pl.ds(dst, cl), _SSD_B0:_SSD_C0] = bm_ref[0, pl.ds(r, cl), :]
        pad_ref[pl.ds(dst, cl), _SSD_C0:_SSD_W] = cm_ref[0, pl.ds(r, cl), :]
        return carry

    lax.fori_loop(0, n_chunks, copy_body, 0)

    conv_w = jnp.concatenate([wx_ref[...], wb_ref[...], wc_ref[...]], axis=1)
    conv_b = jnp.concatenate([bx_ref[...], bb_ref[...], bc_ref[...]], axis=1)
    par = par_ref[0]
    dt_bias = par[0:1, :]
    a_rate = -jnp.exp(par[1:2, :])
    lane_shift = lax.rem(LANE - 2 * SSD_HPG * g, LANE)
    half_win = SSD_CONV // 2

    def act_body(i, carry):
        r = pl.multiple_of(i * cl, cl)
        base = pad_row(r)
        blk = pad_ref[pl.ds(base - _PAD_ROWS, cl + 2 * _PAD_ROWS), :]
        acc = conv_b
        for k in range(SSD_CONV):
            off = _PAD_ROWS + k - half_win
            acc = acc + conv_w[k:k + 1, :] * blk[off:off + cl, :]
        act_ref[pl.ds(r, cl), :] = acc * jax.nn.sigmoid(acc)
        dtv = _softplus(pltpu.roll(dt_ref[0, pl.ds(r, cl), :], lane_shift, 1) + dt_bias)
        dtv_ref[pl.ds(r, cl), :] = dtv
        av_ref[pl.ds(r, cl), :] = dtv * a_rate
        return carry

    lax.fori_loop(0, n_chunks, act_body, 0)

    ri = lax.broadcasted_iota(jnp.int32, (cl, cl), 0)
    ci = lax.broadcasted_iota(jnp.int32, (cl, cl), 1)
    skip = skip_ref[0, 0:1, :] + skip_ref[0, 1:2, :]

    for rev in (False, True):
        tri = (ri <= ci) if rev else (ri >= ci)
        tri_b = jnp.where(tri, 1.0, 0.0).astype(BF16)
        lane0 = SSD_HPG if rev else 0
        h_ref[...] = jnp.zeros(h_ref.shape, F32)

        def chunk_body(i, carry, rev=rev, tri=tri, tri_b=tri_b, lane0=lane0):
            if rev:
                c = n_chunks - 1 - i
            else:
                c = jnp.where(i < n_ctx_chunks, n_lat_chunks + i, i - n_ctx_chunks)
            r = pl.multiple_of(c * cl, cl)
            a = av_ref[pl.ds(r, cl), :]
            a_hi = a.astype(BF16)
            a_r1 = a - a_hi.astype(F32)
            a_mid = a_r1.astype(BF16)
            a_lo = (a_r1 - a_mid.astype(F32)).astype(BF16)
            cs = _dot(tri_b, a_hi) + _dot(tri_b, a_mid) + _dot(tri_b, a_lo)
            cs_t = cs.T
            total = cs[0:1, :] if rev else cs[cl - 1:cl, :]
            dtc = dtv_ref[pl.ds(r, cl), :]
            bmat = act_ref[pl.ds(r, cl), _SSD_B0:_SSD_C0]
            cmat = act_ref[pl.ds(r, cl), _SSD_C0:_SSD_W]
            b_b = bmat.astype(BF16)
            c_b = cmat.astype(BF16)
            cb = _dot_nt(c_b, b_b)
            bt_b = bmat.T.astype(BF16)
            ys = []
            for j in range(SSD_HPG):
                ln = lane0 + j
                col = cs[:, ln:ln + 1]
                row = cs_t[ln:ln + 1, :]
                lmat = jnp.exp(jnp.where(tri, col - row, -jnp.inf))
                xd = act_ref[pl.ds(r, cl), j * SSD_HD:(j + 1) * SSD_HD] * dtc[:, ln:ln + 1]
                y_diag = _dot((cb * lmat).astype(BF16), xd.astype(BF16))
                h_t = h_ref[j]
                y_off = _dot(c_b, h_t.astype(BF16)) * jnp.exp(col)
                tot = total[:, ln:ln + 1]
                state = _dot(bt_b, (xd * jnp.exp(tot - col)).astype(BF16))
                h_ref[j] = jnp.exp(tot) * h_t + state
                ys.append(y_diag + y_off)
            y_chunk = jnp.concatenate(ys, axis=1)
            if rev:
                y_ref[0, pl.ds(r, cl), :] = y_ref[0, pl.ds(r, cl), :] + y_chunk
            else:
                y_ref[0, pl.ds(r, cl), :] = y_chunk + skip * act_ref[pl.ds(r, cl), 0:_SSD_B0]
            return carry

        lax.fori_loop(0, n_chunks, chunk_body, 0)


def _ssd(xbc, dtr, lw, *, n_lat, n_ctx):
    b, t_all, _ = xbc.shape
    gw = SSD_HPG * SSD_HD
    xs_blk = lambda i, g: (i, 0, g)
    b_blk = lambda i, g: (i, 0, SSD_INNER // SSD_STATE + g)
    c_blk = lambda i, g: (i, 0, SSD_INNER // SSD_STATE + SSD_GROUPS + g)
    return pl.pallas_call(
        functools.partial(_ssd_kernel, n_lat=n_lat, n_ctx=n_ctx),
        grid=(b, SSD_GROUPS),
        in_specs=[
            pl.BlockSpec((1, t_all, gw), xs_blk),
            pl.BlockSpec((1, t_all, SSD_STATE), b_blk),
            pl.BlockSpec((1, t_all, SSD_STATE), c_blk),
            pl.BlockSpec((1, t_all, LANE), lambda i, g: (i, 0, 0)),
            pl.BlockSpec((SSD_CONV, gw), lambda i, g: (0, g)),
            pl.BlockSpec((SSD_CONV, SSD_STATE), lambda i, g: (0, SSD_INNER // SSD_STATE + g)),
            pl.BlockSpec((SSD_CONV, SSD_STATE), lambda i, g: (0, SSD_INNER // SSD_STATE + SSD_GROUPS + g)),
            pl.BlockSpec((1, gw), lambda i, g: (0, g)),
            pl.BlockSpec((1, SSD_STATE), lambda i, g: (0, SSD_INNER // SSD_STATE + g)),
            pl.BlockSpec((1, SSD_STATE), lambda i, g: (0, SSD_INNER // SSD_STATE + SSD_GROUPS + g)),
            pl.BlockSpec((1, 8, LANE), lambda i, g: (g, 0, 0)),
            pl.BlockSpec((1, 2, gw), lambda i, g: (g, 0, 0)),
        ],
        out_specs=pl.BlockSpec((1, t_all, gw), xs_blk),
        out_shape=jax.ShapeDtypeStruct((b, t_all, SSD_INNER), F32),
        scratch_shapes=[
            pltpu.VMEM((t_all + 3 * _PAD_ROWS, _SSD_W), F32),
            pltpu.VMEM((t_all, _SSD_W), F32),
            pltpu.VMEM((t_all, LANE), F32),
            pltpu.VMEM((t_all, LANE), F32),
            pltpu.VMEM((SSD_HPG, SSD_STATE, SSD_HD), F32),
        ],
        compiler_params=_cparams(2),
        name="ssd_scan",
    )(xbc, xbc, xbc, dtr, lw["conv_w"], lw["conv_w"], lw["conv_w"], lw["conv_b"], lw["conv_b"],
      lw["conv_b"], lw["ssd_par"], lw["ssd_skip"])


def _softmax_pv(s_t, v_t):
    m = jnp.max(s_t, axis=0, keepdims=True)
    p = jnp.exp2(s_t - m)
    l = jnp.sum(p, axis=0, keepdims=True)
    return _dot(v_t, p.astype(BF16)) / l


def _mla_attn_kernel(q_ref, k_ref, vt_ref, o_ref, *, n_lat, n_ctx, nq_lat, with_ctx_queries):
    def compute(k_lo, k_hi):
        outs = []
        for i in range(2):
            q = q_ref[0, :, i * HEAD_PAD:(i + 1) * HEAD_PAD]
            k = k_ref[0, k_lo:k_hi, i * HEAD_PAD:(i + 1) * HEAD_PAD]
            v_t = vt_ref[0, i * MLA_V:(i + 1) * MLA_V, k_lo:k_hi]
            outs.append(_softmax_pv(_dot_nt(k, q), v_t))
        o_ref[0] = jnp.concatenate(outs, axis=0).T.astype(BF16)

    if not with_ctx_queries:
        compute(0, n_lat + n_ctx)
    else:
        qi = pl.program_id(2)

        @pl.when(qi < nq_lat)
        def _():
            compute(0, n_lat + n_ctx)

        @pl.when(qi >= nq_lat)
        def _():
            compute(n_lat, n_lat + n_ctx)


def _gqa_attn_kernel(q_ref, k_ref, vt_ref, o_ref, *, n_lat, n_ctx, nq_lat, with_ctx_queries):
    def compute(k_lo, k_hi):
        q2 = q_ref[0]
        tq = q2.shape[0]
        lane = lax.broadcasted_iota(jnp.int32, q2.shape, 1)
        zero = jnp.zeros_like(q2)
        qq = jnp.concatenate([jnp.where(lane < GQA_HD, q2, zero), jnp.where(lane >= GQA_HD, q2, zero)], axis=0)
        o_t = _softmax_pv(_dot_nt(k_ref[0, k_lo:k_hi, :], qq), vt_ref[0, :, k_lo:k_hi])
        o_ref[0] = jnp.concatenate([o_t[:, 0:tq], o_t[:, tq:2 * tq]], axis=0).T.astype(BF16)

    if not with_ctx_queries:
        compute(0, n_lat + n_ctx)
    else:
        qi = pl.program_id(2)

        @pl.when(qi < nq_lat)
        def _():
            compute(0, n_lat + n_ctx)

        @pl.when(qi >= nq_lat)
        def _():
            compute(n_lat, n_lat + n_ctx)


def _attention(kind, q, k, v_t, *, n_lat, n_ctx, with_ctx_queries):
    b, t_all, _ = q.shape
    tq = TOKEN_TILE
    nq_lat = n_lat // tq
    nq = t_all // tq if with_ctx_queries else nq_lat
    n_pairs = MLA_HEADS // 2
    if kind == "mla":
        body = _mla_attn_kernel
        in_specs = [
            pl.BlockSpec((1, tq, 2 * HEAD_PAD), lambda i, p, j: (i, j, p)),
            pl.BlockSpec((1, t_all, 2 * HEAD_PAD), lambda i, p, j: (i, 0, p)),
            pl.BlockSpec((1, 2 * MLA_V, t_all), lambda i, p, j: (i, p, 0)),
        ]
    else:
        body = _gqa_attn_kernel
        in_specs = [
            pl.BlockSpec((1, tq, 2 * GQA_HD), lambda i, p, j: (i, j, p)),
            pl.BlockSpec((1, t_all, 2 * GQA_HD), lambda i, p, j: (i, 0, p // 2)),
            pl.BlockSpec((1, GQA_HD, t_all), lambda i, p, j: (i, p // 2, 0)),
        ]
    return pl.pallas_call(
        functools.partial(body, n_lat=n_lat, n_ctx=n_ctx, nq_lat=nq_lat, with_ctx_queries=with_ctx_queries),
        grid=(b, n_pairs, nq),
        in_specs=in_specs,
        out_specs=pl.BlockSpec((1, tq, LANE), lambda i, p, j: (i, j, p)),
        out_shape=jax.ShapeDtypeStruct((b, t_all, MLA_HEADS * MLA_V), BF16),
        compiler_params=_cparams(3),
        name=kind + "_attention",
    )(q, k, v_t)


def _out_kernel(x_ref, modx_ref, modc_ref, gpre_ref, gpost_ref, omla_ref, ogqa_ref, y_ref, wzg_ref, gssd_ref,
                wmo_ref, wgo_ref, wso_ref, wout_ref, o_ref, *, tm, n_lat):
    t = pl.program_id(1)
    x = x_ref[0]
    shift = _mod_rows(t, tm, n_lat, modx_ref, modc_ref, 3)
    scale = _mod_rows(t, tm, n_lat, modx_ref, modc_ref, 4)
    gate = _mod_rows(t, tm, n_lat, modx_ref, modc_ref, 5)
    u = (_rms(x, gpre_ref[...]) * (1.0 + scale) + shift).astype(BF16)
    zg = _dot(u, wzg_ref[...])
    d = x.shape[1]
    z = zg[:, 0:SSD_INNER]
    y = _rms(y_ref[0] * (z * jax.nn.sigmoid(z)), gssd_ref[...]).astype(BF16)
    g0 = jax.nn.sigmoid(zg[:, SSD_INNER:SSD_INNER + d])
    g1 = jax.nn.sigmoid(zg[:, SSD_INNER + d:SSD_INNER + 2 * d])
    g2 = jax.nn.sigmoid(zg[:, SSD_INNER + 2 * d:SSD_INNER + 3 * d])
    merged = (g0 * _dot(omla_ref[0], wmo_ref[...]) + g1 * _dot(ogqa_ref[0], wgo_ref[...])
              + g2 * _dot(y, wso_ref[...]))
    out = _dot(merged.astype(BF16), wout_ref[...])
    o_ref[0] = x + gate * _rms(out, gpost_ref[...])


def _mixer_out(xc, modx, modc, omla, ogqa, y, lw, *, n_lat, n_rows):
    b, t_all, d = xc.shape
    tm = TOKEN_TILE
    row = lambda w: pl.BlockSpec((1, tm, w), lambda i, j: (i, j, 0))
    res = [lw["wzg"], lw["g_ssd"], lw["w_mla_o"], lw["w_gqa_o"], lw["w_ssd_o"], lw["w_out"]]
    return pl.pallas_call(
        functools.partial(_out_kernel, tm=tm, n_lat=n_lat),
        grid=(b, n_rows // tm),
        in_specs=[row(d), pl.BlockSpec((1, N_MOD, d), lambda i, j: (i, 0, 0)), _resident((N_MOD, d)),
                  _resident((1, d)), _resident((1, d)),
                  row(MLA_HEADS * MLA_V), row(GQA_HEADS * GQA_HD), row(SSD_INNER)]
        + [_resident(w.shape) for w in res],
        out_specs=row(d),
        out_shape=jax.ShapeDtypeStruct((b, t_all, d), F32),
        input_output_aliases={0: 0},
        compiler_params=_cparams(2),
        name="mixer_out",
    )(xc, modx, modc, lw["g_pre1"], lw["g_post1"], omla, ogqa, y, *res)


def _rope_tables(n_lat, n_ctx):
    rows = n_lat // GRID_W
    r = jnp.repeat(jnp.arange(rows, dtype=F32), GRID_W)
    c = jnp.tile(jnp.arange(GRID_W, dtype=F32), rows)

    def angles(rot_dim):
        n_freq = rot_dim // 4
        inv = ROPE_THETA ** (-jnp.arange(n_freq, dtype=F32) / n_freq)
        return jnp.concatenate([r[:, None] * inv, c[:, None] * inv], axis=-1)

    def finish(cos_l, sin_l):
        ident_c = jnp.ones((n_ctx, LANE), F32)
        ident_s = jnp.zeros((n_ctx, LANE), F32)
        return jnp.concatenate([cos_l, ident_c], 0), jnp.concatenate([sin_l, ident_s], 0)

    ang = angles(MLA_ROPE)
    cos, sin = jnp.cos(ang), jnp.sin(ang)
    ones = jnp.ones((n_lat, MLA_NOPE), F32)
    zeros = jnp.zeros((n_lat, MLA_NOPE), F32)
    tail1 = jnp.ones((n_lat, HEAD_PAD - MLA_NOPE - MLA_ROPE), F32)
    tail0 = jnp.zeros((n_lat, HEAD_PAD - MLA_NOPE - MLA_ROPE), F32)
    mla = finish(jnp.concatenate([ones, cos, cos, tail1], 1), jnp.concatenate([zeros, -sin, sin, tail0], 1))

    ang = angles(GQA_HD)
    cos, sin = jnp.cos(ang), jnp.sin(ang)
    gqa = finish(jnp.concatenate([cos, cos, cos, cos], 1), jnp.concatenate([-sin, sin, -sin, sin], 1))
    return mla + gqa


def _head_indicators():
    n = GQA_HEADS * GQA_HD
    head = jnp.arange(n) // GQA_HD
    e = (head[:, None] == jnp.arange(LANE)[None, :])
    return (e.astype(F32) / GQA_HD).astype(BF16), e.T.astype(BF16)


def _group_lanes(v):
    return v.reshape(2, SSD_GROUPS, SSD_HPG).transpose(1, 0, 2).reshape(SSD_GROUPS, 2 * SSD_HPG)


def _layer_weights(l, p):
    d = p["w_in"].shape[1]
    w = p["w_in"][l]
    kg = w[:, _C_KG:_C_VG].reshape(d, GQA_KV_HEADS, GQA_HD)
    kdup = jnp.concatenate([kg, kg], axis=-1).reshape(d, 2 * GQA_KV_HEADS * GQA_HD)
    kr = jnp.zeros((d, HEAD_PAD), F32).at[:, MLA_NOPE:MLA_NOPE + MLA_ROPE].set(w[:, _C_KROPE:_C_KG])
    wdt = w[:, _C_DT:_C_QLAT].reshape(d, 2, SSD_GROUPS, SSD_HPG).transpose(0, 2, 1, 3).reshape(d, 2 * SSD_HEADS)
    wdt = jnp.pad(wdt, ((0, 0), (0, LANE - 2 * SSD_HEADS)))
    wa = jnp.concatenate([w[:, _C_KVLAT:_C_KROPE], kdup, w[:, _C_XBC:_C_DT], w[:, _C_QLAT:_C_QG],
                          w[:, _C_QG:_C_Z], kr, wdt], axis=1).astype(BF16)

    wkv = p["w_mla_kv_up"][l].reshape(MLA_KV_RANK, MLA_HEADS, MLA_NOPE + MLA_V)
    wk = jnp.pad(wkv[:, :, :MLA_NOPE], ((0, 0), (0, 0), (0, HEAD_PAD - MLA_NOPE)))
    wk = wk.reshape(MLA_KV_RANK, MLA_HEADS * HEAD_PAD).astype(BF16)
    wv_t = wkv[:, :, MLA_NOPE:].reshape(MLA_KV_RANK, MLA_HEADS * MLA_V).T.astype(BF16)
    wq = p["w_mla_q_up"][l].reshape(MLA_Q_RANK, MLA_HEADS, MLA_NOPE + MLA_ROPE)
    wq = jnp.pad(wq, ((0, 0), (0, 0), (0, HEAD_PAD - MLA_NOPE - MLA_ROPE)))
    wq = wq.reshape(MLA_Q_RANK, MLA_HEADS * HEAD_PAD).astype(BF16)
    src = jnp.arange(HEAD_PAD)[:, None]
    dst = jnp.arange(MLA_HEADS * HEAD_PAD)[None, :]
    pk = jnp.logical_and(dst % HEAD_PAD == src,
                         jnp.logical_and(src >= MLA_NOPE, src < MLA_NOPE + MLA_ROPE)).astype(BF16)

    par = jnp.zeros((SSD_GROUPS, 8, LANE), F32)
    par = par.at[:, 0, :2 * SSD_HPG].set(_group_lanes(p["dt_bias"][l]))
    par = par.at[:, 1, :2 * SSD_HPG].set(_group_lanes(p["a_log"][l]))
    skip = jnp.repeat(p["ssd_skip"][l].reshape(2, SSD_GROUPS, SSD_HPG), SSD_HD, axis=2).transpose(1, 0, 2)

    e, e_t = _head_indicators()
    row = lambda v: v.reshape(1, -1).astype(F32)
    ffn = lambda s: (p["w_ffn_gate"][l, s].astype(BF16), p["w_ffn_up"][l, s].astype(BF16),
                     p["w_ffn_down"][l, s].astype(BF16))
    return {
        "ffn0": ffn(0), "ffn1": ffn(1),
        "g_pre0": row(p["g_pre"][l, 0]), "g_pre1": row(p["g_pre"][l, 1]), "g_pre2": row(p["g_pre"][l, 2]),
        "g_post0": row(p["g_post"][l, 0]), "g_post1": row(p["g_post"][l, 1]), "g_post2": row(p["g_post"][l, 2]),
        "wa": wa, "wvg_t": w[:, _C_VG:_C_XBC].T.astype(BF16),
        "g_mla_kv": row(p["g_mla_kv"][l]), "wk": wk, "wv_t": wv_t, "pk": pk,
        "g_mla_q": row(p["g_mla_q"][l]), "wq": wq, "e": e, "e_t": e_t,
        "g_gqa_k": row(jnp.tile(p["g_gqa_k"][l], 2 * GQA_KV_HEADS)),
        "g_gqa_q": row(jnp.tile(p["g_gqa_q"][l], GQA_HEADS)),
        "conv_w": p["conv_w"][l], "conv_b": row(p["conv_b"][l]), "ssd_par": par, "ssd_skip": skip,
        "wzg": w[:, _C_Z:].astype(BF16), "g_ssd": row(p["g_ssd"][l]),
        "w_mla_o": p["w_mla_o"][l].astype(BF16), "w_gqa_o": p["w_gqa_o"][l].astype(BF16),
        "w_ssd_o": p["w_ssd_o"][l].astype(BF16), "w_out": p["w_out"][l].astype(BF16),
    }


def kernel(x, c, ctx, c_ctx, w_mod, b_mod, g_pre, g_post, w_ffn_gate, w_ffn_up, w_ffn_down, w_in, g_mla_q, w_mla_q_up, g_mla_kv, w_mla_kv_up, g_gqa_q, g_gqa_k, conv_w, conv_b, dt_bias, a_log, ssd_skip, g_ssd, w_mla_o, w_gqa_o, w_ssd_o, w_out):
    b, n_lat, d = x.shape
    n_ctx = ctx.shape[1]
    t_all = n_lat + n_ctx
    depth = w_in.shape[0]
    assert n_lat % TOKEN_TILE == 0 and n_ctx % TOKEN_TILE == 0 and n_lat % GRID_W == 0
    assert w_in.shape[2] == _C_GATE + 3 * d and d == SSD_INNER
    p = dict(g_pre=g_pre, g_post=g_post, w_ffn_gate=w_ffn_gate, w_ffn_up=w_ffn_up, w_ffn_down=w_ffn_down,
             w_in=w_in, g_mla_q=g_mla_q, w_mla_q_up=w_mla_q_up, g_mla_kv=g_mla_kv, w_mla_kv_up=w_mla_kv_up,
             g_gqa_q=g_gqa_q, g_gqa_k=g_gqa_k, conv_w=conv_w, conv_b=conv_b, dt_bias=dt_bias, a_log=a_log,
             ssd_skip=ssd_skip, g_ssd=g_ssd, w_mla_o=w_mla_o, w_gqa_o=w_gqa_o, w_ssd_o=w_ssd_o, w_out=w_out)

    rows = -(-(b + 1) // 8) * 8
    c_all = jnp.concatenate([c, c_ctx[None, :], jnp.zeros((rows - b - 1, d), F32)], axis=0)
    mod = _modulation(c_all, w_mod, b_mod).reshape(depth, rows, N_MOD, d)
    tabs = _rope_tables(n_lat, n_ctx)
    xc = jnp.concatenate([x, ctx], axis=1)

    for l in range(depth):
        last = l == depth - 1
        lw = _layer_weights(l, p)
        modx, modc = mod[l, :b], mod[l, b]
        xc = _ffn(xc, modx, modc, lw["g_pre0"], lw["g_post0"], *lw["ffn0"],
                  k0=0, n_lat=n_lat, n_rows=t_all, alias=True)
        kmla, vtmla, qmla, kgqa, vtgqa, qgqa, xbc, dtr = _in_proj(xc, modx, modc, lw, tabs, n_lat=n_lat)
        y = _ssd(xbc, dtr, lw, n_lat=n_lat, n_ctx=n_ctx)
        omla = _attention("mla", qmla, kmla, vtmla, n_lat=n_lat, n_ctx=n_ctx, with_ctx_queries=not last)
        ogqa = _attention("gqa", qgqa, kgqa, vtgqa, n_lat=n_lat, n_ctx=n_ctx, with_ctx_queries=not last)
        n_rows = n_lat if last else t_all
        xc = _mixer_out(xc, modx, modc, omla, ogqa, y, lw, n_lat=n_lat, n_rows=n_rows)
        xc = _ffn(xc, modx, modc, lw["g_pre2"], lw["g_post2"], *lw["ffn1"],
                  k0=6, n_lat=n_lat, n_rows=n_rows, alias=not last)
    return xc
```

```python
import functools
import math

import jax
import jax.numpy as jnp
from jax import lax
from jax.experimental import pallas as pl
from jax.experimental.pallas import tpu as pltpu

F32 = jnp.float32
BF16 = jnp.bfloat16

GRID_W = 64
ROPE_THETA = 10000.0
NORM_EPS = 1e-6
FFN_RES = 0.5
N_MOD = 9

MLA_HEADS = 16
MLA_NOPE = 64
MLA_ROPE = 32
MLA_V = 64
MLA_Q_RANK = 512
MLA_KV_RANK = 256
GQA_HEADS = 16
GQA_KV_HEADS = 4
GQA_HD = 64
SSD_INNER = 1024
SSD_HD = 64
SSD_HEADS = 16
SSD_GROUPS = 4
SSD_HPG = 4
SSD_STATE = 128
SSD_CONV = 5
SSD_CHUNK = 128
SSD_CONV_DIM = SSD_INNER + 2 * SSD_GROUPS * SSD_STATE
MLA_SCALE = (MLA_NOPE + MLA_ROPE) ** -0.5
GQA_SCALE = GQA_HD ** -0.5
LOG2E = math.log2(math.e)

LANE = 128
HEAD_PAD = 128
TOKEN_TILE = 256
KEY_CHUNK = 256
ATTN_HEADS = 4
ATTN_STAGES = (4, 8, 12)
VMEM_LIMIT = 56 * 1024 * 1024

_C_KVLAT = 0
_C_KROPE = _C_KVLAT + MLA_KV_RANK
_C_KG = _C_KROPE + MLA_ROPE
_C_VG = _C_KG + GQA_KV_HEADS * GQA_HD
_C_XBC = _C_VG + GQA_KV_HEADS * GQA_HD
_C_DT = _C_XBC + SSD_CONV_DIM
_C_QLAT = _C_DT + 2 * SSD_HEADS
_C_QG = _C_QLAT + MLA_Q_RANK
_C_Z = _C_QG + GQA_HEADS * GQA_HD
_C_GATE = _C_Z + SSD_INNER

_A_KVLAT = 0
_A_KDUP = _A_KVLAT + MLA_KV_RANK
_A_XBC = _A_KDUP + 2 * GQA_KV_HEADS * GQA_HD
_A_QLAT = _A_XBC + SSD_CONV_DIM
_A_QG = _A_QLAT + MLA_Q_RANK
_A_KR = _A_QG + GQA_HEADS * GQA_HD
_A_DT = _A_KR + HEAD_PAD
_A_END = _A_DT + LANE


def _cparams(n_axes):
    return pltpu.CompilerParams(
        dimension_semantics=("parallel",) * n_axes, vmem_limit_bytes=VMEM_LIMIT)


def _resident(shape):
    nd = len(shape)
    return pl.BlockSpec(shape, lambda *_: (0,) * nd, pipeline_mode=pl.Buffered(1))


def _rms(t, g):
    return t * lax.rsqrt(jnp.mean(t * t, axis=-1, keepdims=True) + NORM_EPS) * g


def _dot(a, b):
    return jnp.dot(a, b, preferred_element_type=F32)


def _dot_nt(a, b):
    return lax.dot_general(a, b, (((1,), (1,)), ((), ())), preferred_element_type=F32)


def _mod_rows(tile_idx, tm, n_lat, modx_ref, modc_ref, k):
    rows = tile_idx * tm + lax.broadcasted_iota(jnp.int32, (tm, 1), 0)
    return jnp.where(rows >= n_lat, modc_ref[k:k + 1, :], modx_ref[0, k:k + 1, :])


def _mod_kernel(c_ref, w_ref, b_ref, o_ref):
    s = c_ref[...]
    s = s * jax.nn.sigmoid(s)
    o_ref[0] = jnp.dot(s, w_ref[0], preferred_element_type=F32,
                       precision=lax.Precision.HIGHEST) + b_ref[0]


def _modulation(c_all, w_mod, b_mod):
    n_layers, d, _ = w_mod.shape
    rows = c_all.shape[0]
    return pl.pallas_call(
        _mod_kernel,
        grid=(n_layers, N_MOD),
        in_specs=[
            pl.BlockSpec((rows, d), lambda l, j: (0, 0)),
            pl.BlockSpec((1, d, d), lambda l, j: (l, 0, j)),
            pl.BlockSpec((1, 1, d), lambda l, j: (l, 0, j)),
        ],
        out_specs=pl.BlockSpec((1, rows, d), lambda l, j: (l, 0, j)),
        out_shape=jax.ShapeDtypeStruct((n_layers, rows, N_MOD * d), F32),
        compiler_params=_cparams(2),
        name="modulation",
    )(c_all, w_mod, b_mod.reshape(n_layers, 1, N_MOD * d))


def _ffn_kernel(x_ref, modx_ref, modc_ref, gin_ref, gout_ref, wg_ref, wu_ref, wd_ref, o_ref,
                *, tm, n_lat, k0):
    t = pl.program_id(1)
    x = x_ref[0]
    shift = _mod_rows(t, tm, n_lat, modx_ref, modc_ref, k0)
    scale = _mod_rows(t, tm, n_lat, modx_ref, modc_ref, k0 + 1)
    gate = _mod_rows(t, tm, n_lat, modx_ref, modc_ref, k0 + 2)
    u = (_rms(x, gin_ref[...]) * (1.0 + scale) + shift).astype(BF16)
    hg = _dot(u, wg_ref[...])
    hu = _dot(u, wu_ref[...])
    a = (hg * jax.nn.sigmoid(hg) * hu).astype(BF16)
    y = _dot(a, wd_ref[...])
    o_ref[0] = x + (FFN_RES * gate) * _rms(y, gout_ref[...])


def _ffn(xc, modx, modc, g_in, g_out, wg, wu, wd, *, k0, n_lat, n_rows, alias):
    b, t_all, d = xc.shape
    f = wg.shape[1]
    tm = TOKEN_TILE
    out_rows = t_all if alias else n_rows
    return pl.pallas_call(
        functools.partial(_ffn_kernel, tm=tm, n_lat=n_lat, k0=k0),
        grid=(b, n_rows // tm),
        in_specs=[
            pl.BlockSpec((1, tm, d), lambda i, j: (i, j, 0)),
            pl.BlockSpec((1, N_MOD, d), lambda i, j: (i, 0, 0)),
            _resident((N_MOD, d)),
            _resident((1, d)),
            _resident((1, d)),
            _resident((d, f)),
            _resident((d, f)),
            _resident((f, d)),
        ],
        out_specs=pl.BlockSpec((1, tm, d), lambda i, j: (i, j, 0)),
        out_shape=jax.ShapeDtypeStruct((b, out_rows, d), F32),
        input_output_aliases={0: 0} if alias else {},
        compiler_params=_cparams(2),
        name="ffn_half_step",
    )(xc, modx, modc, g_in, g_out, wg, wu, wd)


def _tile_lanes(t, reps):
    return t if reps == 1 else jnp.concatenate([t] * reps, axis=1)


def _rope_gqa(x, cos_t, sin_t):
    w = x.shape[1]
    reps = w // LANE
    lane = lax.broadcasted_iota(jnp.int32, x.shape, 1) % GQA_HD
    half = GQA_HD // 2
    partner = jnp.where(lane < half, pltpu.roll(x, w - half, 1), pltpu.roll(x, half, 1))
    return x * _tile_lanes(cos_t, reps) + partner * _tile_lanes(sin_t, reps)


def _rope_mla(x, cos_t, sin_t):
    w = x.shape[1]
    reps = w // LANE
    lane = lax.broadcasted_iota(jnp.int32, x.shape, 1) % HEAD_PAD
    half = MLA_ROPE // 2
    first = jnp.logical_and(lane >= MLA_NOPE, lane < MLA_NOPE + half)
    partner = jnp.where(first, pltpu.roll(x, w - half, 1), pltpu.roll(x, half, 1))
    return x * _tile_lanes(cos_t, reps) + partner * _tile_lanes(sin_t, reps)


def _head_norm(x, e, e_t, gain):
    ms = _dot((x * x).astype(BF16), e)
    r = lax.rsqrt(ms + NORM_EPS)
    r_hi = r.astype(BF16)
    r_lo = (r - r_hi.astype(F32)).astype(BF16)
    return x * (_dot(r_hi, e_t) + _dot(r_lo, e_t)) * gain


def _in_proj_kernel(x_ref, modx_ref, modc_ref, gpre_ref, wa_ref, wvg_t_ref, gkv_ref, wk_ref, wv_t_ref,
                    pk_ref, gq_ref, wq_ref, e_ref, e_t_ref, ggk_ref, ggq_ref,
                    mla_cos_ref, mla_sin_ref, gqa_cos_ref, gqa_sin_ref,
                    kmla_ref, vtmla_ref, qmla_ref, kgqa_ref, vtgqa_ref, qgqa_ref, xbc_ref, dt_ref,
                    *, tm, n_lat):
    t = pl.program_id(1)
    x = x_ref[0]
    shift = _mod_rows(t, tm, n_lat, modx_ref, modc_ref, 3)
    scale = _mod_rows(t, tm, n_lat, modx_ref, modc_ref, 4)
    u = (_rms(x, gpre_ref[...]) * (1.0 + scale) + shift).astype(BF16)
    h = _dot(u, wa_ref[...])
    xbc_ref[0] = h[:, _A_XBC:_A_QLAT]
    dt_ref[0] = h[:, _A_DT:_A_END]

    mla_cos = mla_cos_ref[...]
    mla_sin = mla_sin_ref[...]
    gqa_cos = gqa_cos_ref[...]
    gqa_sin = gqa_sin_ref[...]

    ckv = _rms(h[:, _A_KVLAT:_A_KDUP], gkv_ref[...]).astype(BF16)
    k_rope = _rope_mla(h[:, _A_KR:_A_DT], mla_cos, mla_sin).astype(BF16)
    kmla_ref[0] = (_dot(ckv, wk_ref[...]) + _dot(k_rope, pk_ref[...])).astype(BF16)
    vtmla_ref[0] = _dot_nt(wv_t_ref[...], ckv).astype(BF16)

    cq = _rms(h[:, _A_QLAT:_A_QG], gq_ref[...]).astype(BF16)
    q = _rope_mla(_dot(cq, wq_ref[...]), mla_cos, mla_sin)
    qmla_ref[0] = (q * (MLA_SCALE * LOG2E)).astype(BF16)

    n_k = 2 * GQA_KV_HEADS * GQA_HD
    kd = _head_norm(h[:, _A_KDUP:_A_XBC], e_ref[0:n_k, :], e_t_ref[:, 0:n_k], ggk_ref[...])
    kgqa_ref[0] = _rope_gqa(kd, gqa_cos, gqa_sin).astype(BF16)
    qg = _head_norm(h[:, _A_QG:_A_KR], e_ref[...], e_t_ref[...], ggq_ref[...])
    qgqa_ref[0] = (_rope_gqa(qg, gqa_cos, gqa_sin) * (GQA_SCALE * LOG2E)).astype(BF16)
    vtgqa_ref[0] = _dot_nt(wvg_t_ref[...], u).astype(BF16)


def _in_proj(xc, modx, modc, lw, tabs, *, n_lat):
    b, t_all, d = xc.shape
    tm = TOKEN_TILE
    n_qm = MLA_HEADS * HEAD_PAD
    n_kd = 2 * GQA_KV_HEADS * GQA_HD
    n_qg = GQA_HEADS * GQA_HD
    n_vm = MLA_HEADS * MLA_V
    n_vg = GQA_KV_HEADS * GQA_HD
    row = lambda w: pl.BlockSpec((1, tm, w), lambda i, j: (i, j, 0))
    col = lambda h: pl.BlockSpec((1, h, tm), lambda i, j: (i, 0, j))
    tab = pl.BlockSpec((tm, LANE), lambda i, j: (j, 0))
    res = [lw["g_pre1"], lw["wa"], lw["wvg_t"], lw["g_mla_kv"], lw["wk"], lw["wv_t"], lw["pk"],
           lw["g_mla_q"], lw["wq"], lw["e"], lw["e_t"], lw["g_gqa_k"], lw["g_gqa_q"]]
    return pl.pallas_call(
        functools.partial(_in_proj_kernel, tm=tm, n_lat=n_lat),
        grid=(b, t_all // tm),
        in_specs=[row(d), pl.BlockSpec((1, N_MOD, d), lambda i, j: (i, 0, 0)), _resident((N_MOD, d))]
        + [_resident(w.shape) for w in res] + [tab] * 4,
        out_specs=[row(n_qm), col(n_vm), row(n_qm), row(n_kd), col(n_vg), row(n_qg),
                   row(SSD_CONV_DIM), row(LANE)],
        out_shape=[
            jax.ShapeDtypeStruct((b, t_all, n_qm), BF16),
            jax.ShapeDtypeStruct((b, n_vm, t_all), BF16),
            jax.ShapeDtypeStruct((b, t_all, n_qm), BF16),
            jax.ShapeDtypeStruct((b, t_all, n_kd), BF16),
            jax.ShapeDtypeStruct((b, n_vg, t_all), BF16),
            jax.ShapeDtypeStruct((b, t_all, n_qg), BF16),
            jax.ShapeDtypeStruct((b, t_all, SSD_CONV_DIM), F32),
            jax.ShapeDtypeStruct((b, t_all, LANE), F32),
        ],
        compiler_params=_cparams(2),
        name="in_proj",
    )(xc, modx, modc, *res, *tabs)


_SSD_W = SSD_HPG * SSD_HD + 2 * SSD_STATE
_SSD_B0 = SSD_HPG * SSD_HD
_SSD_C0 = _SSD_B0 + SSD_STATE
_PAD_ROWS = 8
SSD_UNROLL = 3


def _softplus(v):
    return jnp.maximum(v, 0.0) + jnp.log1p(jnp.exp(-jnp.abs(v)))


def _ssd_kernel(xs_ref, bm_ref, cm_ref, dt_ref, wx_ref, wb_ref, wc_ref, bx_ref, bb_ref, bc_ref,
                par_ref, skip_ref, y_ref, pad_ref, act_ref, dtv_ref, e_ref, s_ref, h_ref, *, n_lat, n_ctx):
    cl = SSD_CHUNK
    t_all = n_lat + n_ctx
    n_chunks = t_all // cl
    n_ctx_chunks = n_ctx // cl
    n_lat_chunks = n_lat // cl
    g = pl.program_id(1)

    def pad_row(r):
        return pl.multiple_of(r + jnp.where(r >= n_lat, 2 * _PAD_ROWS, _PAD_ROWS), _PAD_ROWS)

    zeros = jnp.zeros((_PAD_ROWS, _SSD_W), F32)
    pad_ref[0:_PAD_ROWS, :] = zeros
    pad_ref[_PAD_ROWS + n_lat:2 * _PAD_ROWS + n_lat, :] = zeros
    pad_ref[2 * _PAD_ROWS + t_all:3 * _PAD_ROWS + t_all, :] = zeros

    def copy_body(i, carry):
        r = pl.multiple_of(i * cl, cl)
        dst = pad_row(r)
        pad_ref[pl.ds(dst, cl), 0:_SSD_B0] = xs_ref[0, pl.ds(r, cl), :]
        pad_ref[pl.ds(dst, cl), _SSD_B0:_SSD_C0] = bm_ref[0, pl.ds(r, cl), :]
        pad_ref[pl.ds(dst, cl), _SSD_C0:_SSD_W] = cm_ref[0, pl.ds(r, cl), :]
        return carry

    lax.fori_loop(0, n_chunks, copy_body, 0)

    conv_w = jnp.concatenate([wx_ref[...], wb_ref[...], wc_ref[...]], axis=1)
    conv_b = jnp.concatenate([bx_ref[...], bb_ref[...], bc_ref[...]], axis=1)
    par = par_ref[0]
    dt_bias = par[0:1, :]
    a_rate = -jnp.exp(par[1:2, :])
    lane_shift = lax.rem(LANE - 2 * SSD_HPG * g, LANE)
    half_win = SSD_CONV // 2

    def act_body(i, carry):
        r = pl.multiple_of(i * cl, cl)
        base = pad_row(r)
        blk = pad_ref[pl.ds(base - _PAD_ROWS, cl + 2 * _PAD_ROWS), :]
        acc = conv_b
        for k in range(SSD_CONV):
            off = _PAD_ROWS + k - half_win
            acc = acc + conv_w[k:k + 1, :] * blk[off:off + cl, :]
        act_ref[pl.ds(r, cl), :] = acc * jax.nn.sigmoid(acc)
        dtv = _softplus(pltpu.roll(dt_ref[0, pl.ds(r, cl), :], lane_shift, 1) + dt_bias)
        dtv_ref[pl.ds(r, cl), :] = dtv
        return carry

    lax.fori_loop(0, n_chunks, act_body, 0)

    gw = SSD_HPG * SSD_HD
    ri = lax.broadcasted_iota(jnp.int32, (cl, cl), 0)
    ci = lax.broadcasted_iota(jnp.int32, (cl, cl), 1)
    tris = [ri >= ci, ri <= ci]
    tri_bs = [jnp.where(t, 1.0, 0.0).astype(BF16) for t in tris]
    sl = lax.broadcasted_iota(jnp.int32, (LANE, gw), 0)
    sc = lax.broadcasted_iota(jnp.int32, (LANE, gw), 1) // SSD_HD
    sels = [jnp.where(sl == sc + d * SSD_HPG, 1.0, 0.0).astype(BF16) for d in range(2)]
    skip = skip_ref[0, 0:1, :] + skip_ref[0, 1:2, :]

    def split2(v):
        hi = v.astype(BF16)
        return hi, (v - hi.astype(F32)).astype(BF16)

    def local_strand(d, r, shared, out):
        dtc = dtv_ref[pl.ds(r, cl), :]
        a_hi, a_lo = split2(dtc * a_rate)
        cs = _dot(tri_bs[d], a_hi) + _dot(tri_bs[d], a_lo)
        yield
        cs_t = cs.T
        c_hi, c_lo = split2(cs)
        dt_hi, dt_lo = split2(dtc)
        cs_w = _dot(c_hi, sels[d]) + _dot(c_lo, sels[d])
        dt_w = _dot(dt_hi, sels[d]) + _dot(dt_lo, sels[d])
        yield
        cb, bt_b = shared
        tot_w = cs_w[0:1, :] if d else cs_w[cl - 1:cl, :]
        e_ref[d, pl.ds(r, cl), :] = jnp.exp(cs_w)
        xd = act_ref[pl.ds(r, cl), 0:_SSD_B0] * dt_w
        s_ref[d, pl.ds(r, cl), :] = _dot(bt_b, (xd * jnp.exp(tot_w - cs_w)).astype(BF16))
        xd_b = xd.astype(BF16)
        yield
        ys = []
        for j in range(SSD_HPG):
            ln = d * SSD_HPG + j
            lmat = jnp.exp(jnp.where(tris[d], cs[:, ln:ln + 1] - cs_t[ln:ln + 1, :], -jnp.inf))
            ys.append(_dot((cb * lmat).astype(BF16), xd_b[:, j * SSD_HD:(j + 1) * SSD_HD]))
            yield
        out.append(jnp.concatenate(ys, axis=1))

    def local_body(it, carry):
        rows = [pl.multiple_of((it * SSD_UNROLL + u) * cl, cl) for u in range(SSD_UNROLL)]
        outs = [[] for _ in rows]
        shared = []
        for r in rows:
            bmat = act_ref[pl.ds(r, cl), _SSD_B0:_SSD_C0]
            c_b = act_ref[pl.ds(r, cl), _SSD_C0:_SSD_W].astype(BF16)
            shared.append((_dot_nt(c_b, bmat.astype(BF16)), bmat.T.astype(BF16)))
        strands = [local_strand(d, r, sh, o) for r, sh, o in zip(rows, shared, outs) for d in range(2)]
        while strands:
            for s in list(strands):
                if next(s, "done") == "done":
                    strands.remove(s)
        for r, o in zip(rows, outs):
            y_ref[0, pl.ds(r, cl), :] = skip * act_ref[pl.ds(r, cl), 0:_SSD_B0] + o[0] + o[1]
        return carry

    lax.fori_loop(0, n_chunks // SSD_UNROLL, local_body, 0)

    h_ref[...] = jnp.zeros(h_ref.shape, F32)

    def scan_body(i, carry):
        c_fwd = jnp.where(i < n_ctx_chunks, n_lat_chunks + i, i - n_ctx_chunks)
        for d, c in ((0, c_fwd), (1, n_chunks - 1 - i)):
            r = pl.multiple_of(c * cl, cl)
            h = h_ref[d]
            c_b = act_ref[pl.ds(r, cl), _SSD_C0:_SSD_W].astype(BF16)
            y_off = _dot(c_b, h.astype(BF16)) * e_ref[d, pl.ds(r, cl), :]
            decay = e_ref[d, pl.ds(r + (0 if d else cl - 1), 1), :]
            h_ref[d] = decay * h + s_ref[d, pl.ds(r, cl), :]
            y_ref[0, pl.ds(r, cl), :] = y_ref[0, pl.ds(r, cl), :] + y_off
        return carry

    lax.fori_loop(0, n_chunks, scan_body, 0)


def _ssd(xbc, dtr, lw, *, n_lat, n_ctx):
    b, t_all, _ = xbc.shape
    gw = SSD_HPG * SSD_HD
    assert (t_all // SSD_CHUNK) % SSD_UNROLL == 0
    xs_blk = lambda i, g: (i, 0, g)
    b_blk = lambda i, g: (i, 0, SSD_INNER // SSD_STATE + g)
    c_blk = lambda i, g: (i, 0, SSD_INNER // SSD_STATE + SSD_GROUPS + g)
    return pl.pallas_call(
        functools.partial(_ssd_kernel, n_lat=n_lat, n_ctx=n_ctx),
        grid=(b, SSD_GROUPS),
        in_specs=[
            pl.BlockSpec((1, t_all, gw), xs_blk),
            pl.BlockSpec((1, t_all, SSD_STATE), b_blk),
            pl.BlockSpec((1, t_all, SSD_STATE), c_blk),
            pl.BlockSpec((1, t_all, LANE), lambda i, g: (i, 0, 0)),
            pl.BlockSpec((SSD_CONV, gw), lambda i, g: (0, g)),
            pl.BlockSpec((SSD_CONV, SSD_STATE), lambda i, g: (0, SSD_INNER // SSD_STATE + g)),
            pl.BlockSpec((SSD_CONV, SSD_STATE), lambda i, g: (0, SSD_INNER // SSD_STATE + SSD_GROUPS + g)),
            pl.BlockSpec((1, gw), lambda i, g: (0, g)),
            pl.BlockSpec((1, SSD_STATE), lambda i, g: (0, SSD_INNER // SSD_STATE + g)),
            pl.BlockSpec((1, SSD_STATE), lambda i, g: (0, SSD_INNER // SSD_STATE + SSD_GROUPS + g)),
            pl.BlockSpec((1, 8, LANE), lambda i, g: (g, 0, 0)),
            pl.BlockSpec((1, 2, gw), lambda i, g: (g, 0, 0)),
        ],
        out_specs=pl.BlockSpec((1, t_all, gw), xs_blk),
        out_shape=jax.ShapeDtypeStruct((b, t_all, SSD_INNER), F32),
        scratch_shapes=[
            pltpu.VMEM((t_all + 3 * _PAD_ROWS, _SSD_W), F32),
            pltpu.VMEM((t_all, _SSD_W), F32),
            pltpu.VMEM((t_all, LANE), F32),
            pltpu.VMEM((2, t_all, gw), F32),
            pltpu.VMEM((2, t_all, gw), F32),
            pltpu.VMEM((2, SSD_STATE, gw), F32),
        ],
        compiler_params=_cparams(2),
        name="ssd_scan",
    )(xbc, xbc, xbc, dtr, lw["conv_w"], lw["conv_w"], lw["conv_w"], lw["conv_b"], lw["conv_b"],
      lw["conv_b"], lw["ssd_par"], lw["ssd_skip"])


def _attend_heads(qs, load_ks, load_vts, k_lo, k_hi):
    n = len(qs)
    tasks = [(h, lo, min(lo + KEY_CHUNK, k_hi)) for lo in range(k_lo, k_hi, KEY_CHUNK) for h in range(n)]
    nt = len(tasks)
    m, l, acc = [None] * n, [None] * n, [None] * n
    s, shift, alpha, p = {}, {}, {}, {}
    d_max, d_exp, d_pv = ATTN_STAGES
    for i in range(nt + d_pv):
        if i < nt:
            h, lo, hi = tasks[i]
            s[i] = _dot_nt(load_ks[h](lo, hi), qs[h])
        j = i - d_max
        if 0 <= j < nt:
            h = tasks[j][0]
            cm = jnp.max(s[j], axis=0, keepdims=True)
            if m[h] is None:
                alpha[j] = None
                m[h] = cm
            else:
                m_new = jnp.maximum(m[h], cm)
                alpha[j] = jnp.exp2(m[h] - m_new)
                m[h] = m_new
            shift[j] = m[h]
        j = i - d_exp
        if 0 <= j < nt:
            h = tasks[j][0]
            pj = jnp.exp2(s.pop(j) - shift.pop(j))
            lj = jnp.sum(pj, axis=0, keepdims=True)
            l[h] = lj if alpha[j] is None else alpha[j] * l[h] + lj
            p[j] = pj.astype(BF16)
        j = i - d_pv
        if 0 <= j < nt:
            h, lo, hi = tasks[j]
            pv = _dot(load_vts[h](lo, hi), p.pop(j))
            a = alpha.pop(j)
            acc[h] = pv if a is None else a * acc[h] + pv
    return [acc[h] / l[h] for h in range(n)]


def _by_query_tile(compute, n_lat, n_ctx, nq_lat, with_ctx_queries):
    if not with_ctx_queries:
        compute(0, n_lat + n_ctx)
        return
    qi = pl.program_id(2)

    @pl.when(qi < nq_lat)
    def _():
        compute(0, n_lat + n_ctx)

    @pl.when(qi >= nq_lat)
    def _():
        compute(n_lat, n_lat + n_ctx)


def _store_heads(o_ref, outs):
    for i in range(0, len(outs), 2):
        pair = jnp.concatenate(outs[i:i + 2], axis=0).T.astype(BF16)
        o_ref[0, :, i // 2 * LANE:(i // 2 + 1) * LANE] = pair


def _mla_attn_kernel(q_ref, k_ref, vt_ref, o_ref, *, n_lat, n_ctx, nq_lat, with_ctx_queries):
    def compute(k_lo, k_hi):
        lanes = [slice(i * HEAD_PAD, (i + 1) * HEAD_PAD) for i in range(ATTN_HEADS)]
        rows = [slice(i * MLA_V, (i + 1) * MLA_V) for i in range(ATTN_HEADS)]
        outs = _attend_heads([q_ref[0, :, ln] for ln in lanes],
                             [lambda lo, hi, ln=ln: k_ref[0, lo:hi, ln] for ln in lanes],
                             [lambda lo, hi, rw=rw: vt_ref[0, rw, lo:hi] for rw in rows], k_lo, k_hi)
        _store_heads(o_ref, outs)

    _by_query_tile(compute, n_lat, n_ctx, nq_lat, with_ctx_queries)


def _gqa_attn_kernel(q_ref, k_ref, vt_ref, o_ref, *, n_lat, n_ctx, nq_lat, with_ctx_queries):
    def compute(k_lo, k_hi):
        qs = []
        for i in range(ATTN_HEADS):
            q2 = q_ref[0, :, i // 2 * LANE:(i // 2 + 1) * LANE]
            lane = lax.broadcasted_iota(jnp.int32, q2.shape, 1)
            sel = (lane < GQA_HD) if i % 2 == 0 else (lane >= GQA_HD)
            qs.append(jnp.where(sel, q2, jnp.zeros_like(q2)))
        outs = _attend_heads(qs, [lambda lo, hi: k_ref[0, lo:hi, :]] * ATTN_HEADS,
                             [lambda lo, hi: vt_ref[0, :, lo:hi]] * ATTN_HEADS, k_lo, k_hi)
        _store_heads(o_ref, outs)

    _by_query_tile(compute, n_lat, n_ctx, nq_lat, with_ctx_queries)


def _attention(kind, q, k, v_t, *, n_lat, n_ctx, with_ctx_queries):
    b, t_all, _ = q.shape
    tq = TOKEN_TILE
    nq_lat = n_lat // tq
    nq = t_all // tq if with_ctx_queries else nq_lat
    nh = ATTN_HEADS
    assert nh == GQA_HEADS // GQA_KV_HEADS and MLA_HEADS % nh == 0
    if kind == "mla":
        body = _mla_attn_kernel
        in_specs = [
            pl.BlockSpec((1, tq, nh * HEAD_PAD), lambda i, p, j: (i, j, p)),
            pl.BlockSpec((1, t_all, nh * HEAD_PAD), lambda i, p, j: (i, 0, p)),
            pl.BlockSpec((1, nh * MLA_V, t_all), lambda i, p, j: (i, p, 0)),
        ]
    else:
        body = _gqa_attn_kernel
        in_specs = [
            pl.BlockSpec((1, tq, nh * GQA_HD), lambda i, p, j: (i, j, p)),
            pl.BlockSpec((1, t_all, 2 * GQA_HD), lambda i, p, j: (i, 0, p)),
            pl.BlockSpec((1, GQA_HD, t_all), lambda i, p, j: (i, p, 0)),
        ]
    return pl.pallas_call(
        functools.partial(body, n_lat=n_lat, n_ctx=n_ctx, nq_lat=nq_lat, with_ctx_queries=with_ctx_queries),
        grid=(b, MLA_HEADS // nh, nq),
        in_specs=in_specs,
        out_specs=pl.BlockSpec((1, tq, nh * MLA_V), lambda i, p, j: (i, j, p)),
        out_shape=jax.ShapeDtypeStruct((b, t_all, MLA_HEADS * MLA_V), BF16),
        compiler_params=_cparams(3),
        name=kind + "_attention",
    )(q, k, v_t)


def _out_kernel(x_ref, modx_ref, modc_ref, gpre_ref, gpost_ref, omla_ref, ogqa_ref, y_ref, wzg_ref, gssd_ref,
                wmo_ref, wgo_ref, wso_ref, wout_ref, o_ref, *, tm, n_lat):
    t = pl.program_id(1)
    x = x_ref[0]
    shift = _mod_rows(t, tm, n_lat, modx_ref, modc_ref, 3)
    scale = _mod_rows(t, tm, n_lat, modx_ref, modc_ref, 4)
    gate = _mod_rows(t, tm, n_lat, modx_ref, modc_ref, 5)
    u = (_rms(x, gpre_ref[...]) * (1.0 + scale) + shift).astype(BF16)
    zg = _dot(u, wzg_ref[...])
    d = x.shape[1]
    z = zg[:, 0:SSD_INNER]
    y = _rms(y_ref[0] * (z * jax.nn.sigmoid(z)), gssd_ref[...]).astype(BF16)
    g0 = jax.nn.sigmoid(zg[:, SSD_INNER:SSD_INNER + d])
    g1 = jax.nn.sigmoid(zg[:, SSD_INNER + d:SSD_INNER + 2 * d])
    g2 = jax.nn.sigmoid(zg[:, SSD_INNER + 2 * d:SSD_INNER + 3 * d])
    merged = (g0 * _dot(omla_ref[0], wmo_ref[...]) + g1 * _dot(ogqa_ref[0], wgo_ref[...])
              + g2 * _dot(y, wso_ref[...]))
    out = _dot(merged.astype(BF16), wout_ref[...])
    o_ref[0] = x + gate * _rms(out, gpost_ref[...])


def _mixer_out(xc, modx, modc, omla, ogqa, y, lw, *, n_lat, n_rows):
    b, t_all, d = xc.shape
    tm = TOKEN_TILE
    row = lambda w: pl.BlockSpec((1, tm, w), lambda i, j: (i, j, 0))
    res = [lw["wzg"], lw["g_ssd"], lw["w_mla_o"], lw["w_gqa_o"], lw["w_ssd_o"], lw["w_out"]]
    return pl.pallas_call(
        functools.partial(_out_kernel, tm=tm, n_lat=n_lat),
        grid=(b, n_rows // tm),
        in_specs=[row(d), pl.BlockSpec((1, N_MOD, d), lambda i, j: (i, 0, 0)), _resident((N_MOD, d)),
                  _resident((1, d)), _resident((1, d)),
                  row(MLA_HEADS * MLA_V), row(GQA_HEADS * GQA_HD), row(SSD_INNER)]
        + [_resident(w.shape) for w in res],
        out_specs=row(d),
        out_shape=jax.ShapeDtypeStruct((b, t_all, d), F32),
        input_output_aliases={0: 0},
        compiler_params=_cparams(2),
        name="mixer_out",
    )(xc, modx, modc, lw["g_pre1"], lw["g_post1"], omla, ogqa, y, *res)


def _rope_tables(n_lat, n_ctx):
    rows = n_lat // GRID_W
    r = jnp.repeat(jnp.arange(rows, dtype=F32), GRID_W)
    c = jnp.tile(jnp.arange(GRID_W, dtype=F32), rows)

    def angles(rot_dim):
        n_freq = rot_dim // 4
        inv = ROPE_THETA ** (-jnp.arange(n_freq, dtype=F32) / n_freq)
        return jnp.concatenate([r[:, None] * inv, c[:, None] * inv], axis=-1)

    def finish(cos_l, sin_l):
        ident_c = jnp.ones((n_ctx, LANE), F32)
        ident_s = jnp.zeros((n_ctx, LANE), F32)
        return jnp.concatenate([cos_l, ident_c], 0), jnp.concatenate([sin_l, ident_s], 0)

    ang = angles(MLA_ROPE)
    cos, sin = jnp.cos(ang), jnp.sin(ang)
    ones = jnp.ones((n_lat, MLA_NOPE), F32)
    zeros = jnp.zeros((n_lat, MLA_NOPE), F32)
    tail1 = jnp.ones((n_lat, HEAD_PAD - MLA_NOPE - MLA_ROPE), F32)
    tail0 = jnp.zeros((n_lat, HEAD_PAD - MLA_NOPE - MLA_ROPE), F32)
    mla = finish(jnp.concatenate([ones, cos, cos, tail1], 1), jnp.concatenate([zeros, -sin, sin, tail0], 1))

    ang = angles(GQA_HD)
    cos, sin = jnp.cos(ang), jnp.sin(ang)
    gqa = finish(jnp.concatenate([cos, cos, cos, cos], 1), jnp.concatenate([-sin, sin, -sin, sin], 1))
    return mla + gqa


def _head_indicators():
    n = GQA_HEADS * GQA_HD
    head = jnp.arange(n) // GQA_HD
    e = (head[:, None] == jnp.arange(LANE)[None, :])
    return (e.astype(F32) / GQA_HD).astype(BF16), e.T.astype(BF16)


def _group_lanes(v):
    return v.reshape(2, SSD_GROUPS, SSD_HPG).transpose(1, 0, 2).reshape(SSD_GROUPS, 2 * SSD_HPG)


def _layer_weights(l, p):
    d = p["w_in"].shape[1]
    w = p["w_in"][l]
    kg = w[:, _C_KG:_C_VG].reshape(d, GQA_KV_HEADS, GQA_HD)
    kdup = jnp.concatenate([kg, kg], axis=-1).reshape(d, 2 * GQA_KV_HEADS * GQA_HD)
    kr = jnp.zeros((d, HEAD_PAD), F32).at[:, MLA_NOPE:MLA_NOPE + MLA_ROPE].set(w[:, _C_KROPE:_C_KG])
    wdt = w[:, _C_DT:_C_QLAT].reshape(d, 2, SSD_GROUPS, SSD_HPG).transpose(0, 2, 1, 3).reshape(d, 2 * SSD_HEADS)
    wdt = jnp.pad(wdt, ((0, 0), (0, LANE - 2 * SSD_HEADS)))
    wa = jnp.concatenate([w[:, _C_KVLAT:_C_KROPE], kdup, w[:, _C_XBC:_C_DT], w[:, _C_QLAT:_C_QG],
                          w[:, _C_QG:_C_Z], kr, wdt], axis=1).astype(BF16)

    wkv = p["w_mla_kv_up"][l].reshape(MLA_KV_RANK, MLA_HEADS, MLA_NOPE + MLA_V)
    wk = jnp.pad(wkv[:, :, :MLA_NOPE], ((0, 0), (0, 0), (0, HEAD_PAD - MLA_NOPE)))
    wk = wk.reshape(MLA_KV_RANK, MLA_HEADS * HEAD_PAD).astype(BF16)
    wv_t = wkv[:, :, MLA_NOPE:].reshape(MLA_KV_RANK, MLA_HEADS * MLA_V).T.astype(BF16)
    wq = p["w_mla_q_up"][l].reshape(MLA_Q_RANK, MLA_HEADS, MLA_NOPE + MLA_ROPE)
    wq = jnp.pad(wq, ((0, 0), (0, 0), (0, HEAD_PAD - MLA_NOPE - MLA_ROPE)))
    wq = wq.reshape(MLA_Q_RANK, MLA_HEADS * HEAD_PAD).astype(BF16)
    src = jnp.arange(HEAD_PAD)[:, None]
    dst = jnp.arange(MLA_HEADS * HEAD_PAD)[None, :]
    pk = jnp.logical_and(dst % HEAD_PAD == src,
                         jnp.logical_and(src >= MLA_NOPE, src < MLA_NOPE + MLA_ROPE)).astype(BF16)

    par = jnp.zeros((SSD_GROUPS, 8, LANE), F32)
    par = par.at[:, 0, :2 * SSD_HPG].set(_group_lanes(p["dt_bias"][l]))
    par = par.at[:, 1, :2 * SSD_HPG].set(_group_lanes(p["a_log"][l]))
    skip = jnp.repeat(p["ssd_skip"][l].reshape(2, SSD_GROUPS, SSD_HPG), SSD_HD, axis=2).transpose(1, 0, 2)

    e, e_t = _head_indicators()
    row = lambda v: v.reshape(1, -1).astype(F32)
    ffn = lambda s: (p["w_ffn_gate"][l, s].astype(BF16), p["w_ffn_up"][l, s].astype(BF16),
                     p["w_ffn_down"][l, s].astype(BF16))
    return {
        "ffn0": ffn(0), "ffn1": ffn(1),
        "g_pre0": row(p["g_pre"][l, 0]), "g_pre1": row(p["g_pre"][l, 1]), "g_pre2": row(p["g_pre"][l, 2]),
        "g_post0": row(p["g_post"][l, 0]), "g_post1": row(p["g_post"][l, 1]), "g_post2": row(p["g_post"][l, 2]),
        "wa": wa, "wvg_t": w[:, _C_VG:_C_XBC].T.astype(BF16),
        "g_mla_kv": row(p["g_mla_kv"][l]), "wk": wk, "wv_t": wv_t, "pk": pk,
        "g_mla_q": row(p["g_mla_q"][l]), "wq": wq, "e": e, "e_t": e_t,
        "g_gqa_k": row(jnp.tile(p["g_gqa_k"][l], 2 * GQA_KV_HEADS)),
        "g_gqa_q": row(jnp.tile(p["g_gqa_q"][l], GQA_HEADS)),
        "conv_w": p["conv_w"][l], "conv_b": row(p["conv_b"][l]), "ssd_par": par, "ssd_skip": skip,
        "wzg": w[:, _C_Z:].astype(BF16), "g_ssd": row(p["g_ssd"][l]),
        "w_mla_o": p["w_mla_o"][l].astype(BF16), "w_gqa_o": p["w_gqa_o"][l].astype(BF16),
        "w_ssd_o": p["w_ssd_o"][l].astype(BF16), "w_out": p["w_out"][l].astype(BF16),
    }


def kernel(x, c, ctx, c_ctx, w_mod, b_mod, g_pre, g_post, w_ffn_gate, w_ffn_up, w_ffn_down, w_in, g_mla_q, w_mla_q_up, g_mla_kv, w_mla_kv_up, g_gqa_q, g_gqa_k, conv_w, conv_b, dt_bias, a_log, ssd_skip, g_ssd, w_mla_o, w_gqa_o, w_ssd_o, w_out):
    b, n_lat, d = x.shape
    n_ctx = ctx.shape[1]
    t_all = n_lat + n_ctx
    depth = w_in.shape[0]
    assert n_lat % TOKEN_TILE == 0 and n_ctx % TOKEN_TILE == 0 and n_lat % GRID_W == 0
    assert w_in.shape[2] == _C_GATE + 3 * d and d == SSD_INNER
    p = dict(g_pre=g_pre, g_post=g_post, w_ffn_gate=w_ffn_gate, w_ffn_up=w_ffn_up, w_ffn_down=w_ffn_down,
             w_in=w_in, g_mla_q=g_mla_q, w_mla_q_up=w_mla_q_up, g_mla_kv=g_mla_kv, w_mla_kv_up=w_mla_kv_up,
             g_gqa_q=g_gqa_q, g_gqa_k=g_gqa_k, conv_w=conv_w, conv_b=conv_b, dt_bias=dt_bias, a_log=a_log,
             ssd_skip=ssd_skip, g_ssd=g_ssd, w_mla_o=w_mla_o, w_gqa_o=w_gqa_o, w_ssd_o=w_ssd_o, w_out=w_out)

    rows = -(-(b + 1) // 8) * 8
    c_all = jnp.concatenate([c, c_ctx[None, :], jnp.zeros((rows - b - 1, d), F32)], axis=0)
    mod = _modulation(c_all, w_mod, b_mod).reshape(depth, rows, N_MOD, d)
    tabs = _rope_tables(n_lat, n_ctx)
    xc = jnp.concatenate([x, ctx], axis=1)

    for l in range(depth):
        last = l == depth - 1
        lw = _layer_weights(l, p)
        modx, modc = mod[l, :b], mod[l, b]
        xc = _ffn(xc, modx, modc, lw["g_pre0"], lw["g_post0"], *lw["ffn0"],
                  k0=0, n_lat=n_lat, n_rows=t_all, alias=True)
        kmla, vtmla, qmla, kgqa, vtgqa, qgqa, xbc, dtr = _in_proj(xc, modx, modc, lw, tabs, n_lat=n_lat)
        y = _ssd(xbc, dtr, lw, n_lat=n_lat, n_ctx=n_ctx)
        omla = _attention("mla", qmla, kmla, vtmla, n_lat=n_lat, n_ctx=n_ctx, with_ctx_queries=not last)
        ogqa = _attention("gqa", qgqa, kgqa, vtgqa, n_lat=n_lat, n_ctx=n_ctx, with_ctx_queries=not last)
        n_rows = n_lat if last else t_all
        xc = _mixer_out(xc, modx, modc, omla, ogqa, y, lw, n_lat=n_lat, n_rows=n_rows)
        xc = _ffn(xc, modx, modc, lw["g_pre2"], lw["g_post2"], *lw["ffn1"],
                  k0=6, n_lat=n_lat, n_rows=n_rows, alias=not last)
    return xc
```

```python
import functools
import math

import jax
import jax.numpy as jnp
from jax import lax
from jax.experimental import pallas as pl
from jax.experimental.pallas import tpu as pltpu

F32 = jnp.float32
BF16 = jnp.bfloat16

GRID_W = 64
ROPE_THETA = 10000.0
NORM_EPS = 1e-6
FFN_RES = 0.5
N_MOD = 9

MLA_HEADS = 16
MLA_NOPE = 64
MLA_ROPE = 32
MLA_V = 64
MLA_Q_RANK = 512
MLA_KV_RANK = 256
GQA_HEADS = 16
GQA_KV_HEADS = 4
GQA_HD = 64
SSD_INNER = 1024
SSD_HD = 64
SSD_HEADS = 16
SSD_GROUPS = 4
SSD_HPG = 4
SSD_STATE = 128
SSD_CONV = 5
SSD_CHUNK = 128
SSD_CONV_DIM = SSD_INNER + 2 * SSD_GROUPS * SSD_STATE
MLA_SCALE = (MLA_NOPE + MLA_ROPE) ** -0.5
GQA_SCALE = GQA_HD ** -0.5
LOG2E = math.log2(math.e)

LANE = 128
HEAD_PAD = 128
TOKEN_TILE = 256
SUB_TILE = 256
OUT_SUB = 256
FFN_COLS = 768
KEY_CHUNK = 256
ATTN_HEADS = 4
SUM_ROWS = 16
ATTN_STAGES = (4, 8, 12)
VMEM_LIMIT = 56 * 1024 * 1024

_C_KVLAT = 0
_C_KROPE = _C_KVLAT + MLA_KV_RANK
_C_KG = _C_KROPE + MLA_ROPE
_C_VG = _C_KG + GQA_KV_HEADS * GQA_HD
_C_XBC = _C_VG + GQA_KV_HEADS * GQA_HD
_C_DT = _C_XBC + SSD_CONV_DIM
_C_QLAT = _C_DT + 2 * SSD_HEADS
_C_QG = _C_QLAT + MLA_Q_RANK
_C_Z = _C_QG + GQA_HEADS * GQA_HD
_C_GATE = _C_Z + SSD_INNER

_A_KVLAT = 0
_A_KDUP = _A_KVLAT + MLA_KV_RANK
_A_XBC = _A_KDUP + 2 * GQA_KV_HEADS * GQA_HD
_A_QLAT = _A_XBC + SSD_CONV_DIM
_A_QG = _A_QLAT + MLA_Q_RANK
_A_KR = _A_QG + GQA_HEADS * GQA_HD
_A_DT = _A_KR + HEAD_PAD
_A_END = _A_DT + LANE


def _cparams(n_axes):
    return pltpu.CompilerParams(
        dimension_semantics=("parallel",) * n_axes, vmem_limit_bytes=VMEM_LIMIT)


def _resident(shape):
    nd = len(shape)
    return pl.BlockSpec(shape, lambda *_: (0,) * nd, pipeline_mode=pl.Buffered(1))


def _rms(t, g):
    return t * lax.rsqrt(jnp.mean(t * t, axis=-1, keepdims=True) + NORM_EPS) * g


def _dot(a, b):
    return jnp.dot(a, b, preferred_element_type=F32)


def _dot_nt(a, b):
    return lax.dot_general(a, b, (((1,), (1,)), ((), ())), preferred_element_type=F32)


def _mod_rows(tile_idx, tm, n_lat, modx_ref, modc_ref, k, row0=0, n_rows=None):
    n_rows = tm if n_rows is None else n_rows
    rows = tile_idx * tm + row0 + lax.broadcasted_iota(jnp.int32, (n_rows, 1), 0)
    return jnp.where(rows >= n_lat, modc_ref[k:k + 1, :], modx_ref[0, k:k + 1, :])


def _emit_skewed(strands):
    pending, active = list(strands), []
    while pending or active:
        if pending:
            active.append(pending.pop(0))
        for s in list(active):
            if next(s, "done") == "done":
                active.remove(s)


def _mod_kernel(c_ref, w_ref, b_ref, o_ref):
    s = c_ref[...]
    s = s * jax.nn.sigmoid(s)
    o_ref[0] = jnp.dot(s, w_ref[0], preferred_element_type=F32,
                       precision=lax.Precision.HIGHEST) + b_ref[0]


def _modulation(c_all, w_mod, b_mod):
    n_layers, d, _ = w_mod.shape
    rows = c_all.shape[0]
    return pl.pallas_call(
        _mod_kernel,
        grid=(n_layers, N_MOD),
        in_specs=[
            pl.BlockSpec((rows, d), lambda l, j: (0, 0)),
            pl.BlockSpec((1, d, d), lambda l, j: (l, 0, j)),
            pl.BlockSpec((1, 1, d), lambda l, j: (l, 0, j)),
        ],
        out_specs=pl.BlockSpec((1, rows, d), lambda l, j: (l, 0, j)),
        out_shape=jax.ShapeDtypeStruct((n_layers, rows, N_MOD * d), F32),
        compiler_params=_cparams(2),
        name="modulation",
    )(c_all, w_mod, b_mod.reshape(n_layers, 1, N_MOD * d))


def _ffn_kernel(x_ref, modx_ref, modc_ref, gin_ref, gout_ref, wg_ref, wu_ref, wd_ref, o_ref,
                *, tm, n_lat, k0):
    t = pl.program_id(1)
    f = wg_ref.shape[1]
    cuts = list(range(0, f, FFN_COLS)) + [f]

    def strand(r0):
        rows = slice(r0, r0 + SUB_TILE)
        mod = lambda k: _mod_rows(t, tm, n_lat, modx_ref, modc_ref, k, r0, SUB_TILE)
        x = x_ref[0, rows, :]
        u = (_rms(x, gin_ref[...]) * (1.0 + mod(k0 + 1)) + mod(k0)).astype(BF16)
        yield
        hidden = []
        for lo, hi in zip(cuts[:-1], cuts[1:]):
            hidden.append((_dot(u, wg_ref[:, lo:hi]), _dot(u, wu_ref[:, lo:hi])))
            yield
        y = None
        for (hg, hu), lo, hi in zip(hidden, cuts[:-1], cuts[1:]):
            a = (hg * jax.nn.sigmoid(hg) * hu).astype(BF16)
            part = _dot(a, wd_ref[lo:hi, :])
            y = part if y is None else y + part
            yield
        o_ref[0, rows, :] = x + (FFN_RES * mod(k0 + 2)) * _rms(y, gout_ref[...])

    _emit_skewed([strand(r0) for r0 in range(0, tm, SUB_TILE)])


def _row_tile(n_rows):
    for tm in (3 * SUB_TILE, 2 * SUB_TILE):
        if n_rows % tm == 0:
            return tm
    return SUB_TILE


def _ffn(xc, modx, modc, g_in, g_out, wg, wu, wd, *, k0, n_lat, n_rows, alias):
    b, t_all, d = xc.shape
    f = wg.shape[1]
    tm = _row_tile(n_rows)
    out_rows = t_all if alias else n_rows
    return pl.pallas_call(
        functools.partial(_ffn_kernel, tm=tm, n_lat=n_lat, k0=k0),
        grid=(b, n_rows // tm),
        in_specs=[
            pl.BlockSpec((1, tm, d), lambda i, j: (i, j, 0)),
            pl.BlockSpec((1, N_MOD, d), lambda i, j: (i, 0, 0)),
            _resident((N_MOD, d)),
            _resident((1, d)),
            _resident((1, d)),
            _resident((d, f)),
            _resident((d, f)),
            _resident((f, d)),
        ],
        out_specs=pl.BlockSpec((1, tm, d), lambda i, j: (i, j, 0)),
        out_shape=jax.ShapeDtypeStruct((b, out_rows, d), F32),
        input_output_aliases={0: 0} if alias else {},
        compiler_params=_cparams(2),
        name="ffn_half_step",
    )(xc, modx, modc, g_in, g_out, wg, wu, wd)


def _tile_lanes(t, reps):
    return t if reps == 1 else jnp.concatenate([t] * reps, axis=1)


def _rope_gqa(x, cos_t, sin_t):
    w = x.shape[1]
    reps = w // LANE
    lane = lax.broadcasted_iota(jnp.int32, x.shape, 1) % GQA_HD
    half = GQA_HD // 2
    partner = jnp.where(lane < half, pltpu.roll(x, w - half, 1), pltpu.roll(x, half, 1))
    return x * _tile_lanes(cos_t, reps) + partner * _tile_lanes(sin_t, reps)


def _rope_mla(x, cos_t, sin_t):
    w = x.shape[1]
    reps = w // LANE
    lane = lax.broadcasted_iota(jnp.int32, x.shape, 1) % HEAD_PAD
    half = MLA_ROPE // 2
    first = jnp.logical_and(lane >= MLA_NOPE, lane < MLA_NOPE + half)
    partner = jnp.where(first, pltpu.roll(x, w - half, 1), pltpu.roll(x, half, 1))
    return x * _tile_lanes(cos_t, reps) + partner * _tile_lanes(sin_t, reps)


def _head_scale(x, ms, gain):
    r = lax.rsqrt(ms + NORM_EPS)
    lane = lax.broadcasted_iota(jnp.int32, r.shape, 1)
    second = (lane >= GQA_HD).astype(jnp.int32)
    tiles = [jnp.take_along_axis(r, 2 * j + second, axis=1) for j in range(x.shape[1] // LANE)]
    return x * jnp.concatenate(tiles, axis=1) * gain


def _in_proj_kernel(x_ref, modx_ref, modc_ref, gpre_ref, wa_ref, wvg_t_ref, gkv_ref, wk_ref, wv_t_ref,
                    gq_ref, wq_ref, e_ref, ggk_ref, ggq_ref,
                    mla_cos_ref, mla_sin_ref, gqa_cos_ref, gqa_sin_ref,
                    kmla_ref, vtmla_ref, qmla_ref, kgqa_ref, vtgqa_ref, qgqa_ref, xbc_ref, dt_ref,
                    *, tm, n_lat):
    t = pl.program_id(1)
    x = x_ref[0]
    shift = _mod_rows(t, tm, n_lat, modx_ref, modc_ref, 3)
    scale = _mod_rows(t, tm, n_lat, modx_ref, modc_ref, 4)
    u = (_rms(x, gpre_ref[...]) * (1.0 + scale) + shift).astype(BF16)
    proj = lambda lo, hi: _dot(u, wa_ref[:, lo:hi])
    mla_cos = mla_cos_ref[...]
    mla_sin = mla_sin_ref[...]
    gqa_cos = gqa_cos_ref[...]
    gqa_sin = gqa_sin_ref[...]
    n_k = 2 * GQA_KV_HEADS * GQA_HD

    h_kv = proj(_A_KVLAT, _A_KDUP)
    h_q = proj(_A_QLAT, _A_QG)
    h_kd = proj(_A_KDUP, _A_XBC)
    h_qg = proj(_A_QG, _A_KR)
    h_kr = proj(_A_KR, _A_END)
    ckv = _rms(h_kv, gkv_ref[...]).astype(BF16)
    cq = _rms(h_q, gq_ref[...]).astype(BF16)
    k_nope = _dot(ckv, wk_ref[...])
    v_t = _dot_nt(wv_t_ref[...], ckv)
    q_mla = _dot(cq, wq_ref[...])
    ms_k = _dot((h_kd * h_kd).astype(BF16), e_ref[0:n_k, :])
    ms_q = _dot((h_qg * h_qg).astype(BF16), e_ref[...])
    vg_t = _dot_nt(wvg_t_ref[...], u)

    dt_ref[0] = h_kr[:, HEAD_PAD:]
    k_rope = _rope_mla(h_kr[:, 0:HEAD_PAD], mla_cos, mla_sin)
    kmla_ref[0] = (k_nope + _tile_lanes(k_rope, MLA_HEADS)).astype(BF16)
    vtmla_ref[0] = v_t.astype(BF16)
    qmla_ref[0] = (_rope_mla(q_mla, mla_cos, mla_sin) * (MLA_SCALE * LOG2E)).astype(BF16)
    kd = _head_scale(h_kd, ms_k, ggk_ref[...])
    kgqa_ref[0] = _rope_gqa(kd, gqa_cos, gqa_sin).astype(BF16)
    qg = _head_scale(h_qg, ms_q, ggq_ref[...])
    qgqa_ref[0] = (_rope_gqa(qg, gqa_cos, gqa_sin) * (GQA_SCALE * LOG2E)).astype(BF16)
    vtgqa_ref[0] = vg_t.astype(BF16)
    xbc_ref[0] = proj(_A_XBC, _A_QLAT)


def _in_proj(xc, modx, modc, lw, tabs, *, n_lat):
    b, t_all, d = xc.shape
    tm = TOKEN_TILE
    n_qm = MLA_HEADS * HEAD_PAD
    n_kd = 2 * GQA_KV_HEADS * GQA_HD
    n_qg = GQA_HEADS * GQA_HD
    n_vm = MLA_HEADS * MLA_V
    n_vg = GQA_KV_HEADS * GQA_HD
    row = lambda w: pl.BlockSpec((1, tm, w), lambda i, j: (i, j, 0))
    col = lambda h: pl.BlockSpec((1, h, tm), lambda i, j: (i, 0, j))
    tab = pl.BlockSpec((tm, LANE), lambda i, j: (j, 0))
    res = [lw["g_pre1"], lw["wa"], lw["wvg_t"], lw["g_mla_kv"], lw["wk"], lw["wv_t"],
           lw["g_mla_q"], lw["wq"], lw["e"], lw["g_gqa_k"], lw["g_gqa_q"]]
    return pl.pallas_call(
        functools.partial(_in_proj_kernel, tm=tm, n_lat=n_lat),
        grid=(b, t_all // tm),
        in_specs=[row(d), pl.BlockSpec((1, N_MOD, d), lambda i, j: (i, 0, 0)), _resident((N_MOD, d))]
        + [_resident(w.shape) for w in res] + [tab] * 4,
        out_specs=[row(n_qm), col(n_vm), row(n_qm), row(n_kd), col(n_vg), row(n_qg),
                   row(SSD_CONV_DIM), row(LANE)],
        out_shape=[
            jax.ShapeDtypeStruct((b, t_all, n_qm), BF16),
            jax.ShapeDtypeStruct((b, n_vm, t_all), BF16),
            jax.ShapeDtypeStruct((b, t_all, n_qm), BF16),
            jax.ShapeDtypeStruct((b, t_all, n_kd), BF16),
            jax.ShapeDtypeStruct((b, n_vg, t_all), BF16),
            jax.ShapeDtypeStruct((b, t_all, n_qg), BF16),
            jax.ShapeDtypeStruct((b, t_all, SSD_CONV_DIM), F32),
            jax.ShapeDtypeStruct((b, t_all, LANE), F32),
        ],
        compiler_params=_cparams(2),
        name="in_proj",
    )(xc, modx, modc, *res, *tabs)


_SSD_W = SSD_HPG * SSD_HD + 2 * SSD_STATE
_SSD_B0 = SSD_HPG * SSD_HD
_SSD_C0 = _SSD_B0 + SSD_STATE
_PAD_ROWS = 8
SSD_UNROLL = 3


def _softplus(v):
    return jnp.maximum(v, 0.0) + jnp.log1p(jnp.exp(-jnp.abs(v)))


def _ssd_kernel(xs_ref, bm_ref, cm_ref, dt_ref, wx_ref, wb_ref, wc_ref, bx_ref, bb_ref, bc_ref,
                par_ref, skip_ref, y_ref, pad_ref, act_ref, dtv_ref, e_ref, s_ref, h_ref, *, n_lat, n_ctx):
    cl = SSD_CHUNK
    t_all = n_lat + n_ctx
    n_chunks = t_all // cl
    n_ctx_chunks = n_ctx // cl
    n_lat_chunks = n_lat // cl
    g = pl.program_id(1)

    def pad_row(r):
        return pl.multiple_of(r + jnp.where(r >= n_lat, 2 * _PAD_ROWS, _PAD_ROWS), _PAD_ROWS)

    zeros = jnp.zeros((_PAD_ROWS, _SSD_W), F32)
    pad_ref[0:_PAD_ROWS, :] = zeros
    pad_ref[_PAD_ROWS + n_lat:2 * _PAD_ROWS + n_lat, :] = zeros
    pad_ref[2 * _PAD_ROWS + t_all:3 * _PAD_ROWS + t_all, :] = zeros

    def copy_body(i, carry):
        r = pl.multiple_of(i * cl, cl)
        dst = pad_row(r)
        pad_ref[pl.ds(dst, cl), 0:_SSD_B0] = xs_ref[0, pl.ds(r, cl), :]
        pad_ref[pl.ds(dst, cl), _SSD_B0:_SSD_C0] = bm_ref[0, pl.ds(r, cl), :]
        pad_ref[pl.ds(dst, cl), _SSD_C0:_SSD_W] = cm_ref[0, pl.ds(r, cl), :]
        return carry

    lax.fori_loop(0, n_chunks, copy_body, 0)

    conv_w = jnp.concatenate([wx_ref[...], wb_ref[...], wc_ref[...]], axis=1)
    conv_b = jnp.concatenate([bx_ref[...], bb_ref[...], bc_ref[...]], axis=1)
    par = par_ref[0]
    dt_bias = par[0:1, :]
    a_rate = -jnp.exp(par[1:2, :])
    lane_shift = lax.rem(LANE - 2 * SSD_HPG * g, LANE)
    half_win = SSD_CONV // 2

    def act_body(i, carry):
        r = pl.multiple_of(i * cl, cl)
        base = pad_row(r)
        blk = pad_ref[pl.ds(base - _PAD_ROWS, cl + 2 * _PAD_ROWS), :]
        acc = conv_b
        for k in range(SSD_CONV):
            off = _PAD_ROWS + k - half_win
            acc = acc + conv_w[k:k + 1, :] * blk[off:off + cl, :]
        act_ref[pl.ds(r, cl), :] = acc * jax.nn.sigmoid(acc)
        dtv = _softplus(pltpu.roll(dt_ref[0, pl.ds(r, cl), :], lane_shift, 1) + dt_bias)
        dtv_ref[pl.ds(r, cl), :] = dtv
        return carry

    lax.fori_loop(0, n_chunks, act_body, 0)

    gw = SSD_HPG * SSD_HD
    ri = lax.broadcasted_iota(jnp.int32, (cl, cl), 0)
    ci = lax.broadcasted_iota(jnp.int32, (cl, cl), 1)
    tris = [ri >= ci, ri <= ci]
    tri_bs = [jnp.where(t, 1.0, 0.0).astype(BF16) for t in tris]
    sl = lax.broadcasted_iota(jnp.int32, (LANE, gw), 0)
    sc = lax.broadcasted_iota(jnp.int32, (LANE, gw), 1) // SSD_HD
    sels = [jnp.where(sl == sc + d * SSD_HPG, 1.0, 0.0).astype(BF16) for d in range(2)]
    skip = skip_ref[0, 0:1, :] + skip_ref[0, 1:2, :]

    def split2(v):
        hi = v.astype(BF16)
        return hi, (v - hi.astype(F32)).astype(BF16)

    def local_strand(d, r, shared, out):
        dtc = dtv_ref[pl.ds(r, cl), :]
        a_hi, a_lo = split2(dtc * a_rate)
        cs = _dot(tri_bs[d], a_hi) + _dot(tri_bs[d], a_lo)
        yield
        cs_t = cs.T
        c_hi, c_lo = split2(cs)
        dt_hi, dt_lo = split2(dtc)
        cs_w = _dot(c_hi, sels[d]) + _dot(c_lo, sels[d])
        dt_w = _dot(dt_hi, sels[d]) + _dot(dt_lo, sels[d])
        yield
        cb, bt_b = shared
        tot_w = cs_w[0:1, :] if d else cs_w[cl - 1:cl, :]
        e_ref[d, pl.ds(r, cl), :] = jnp.exp(cs_w)
        xd = act_ref[pl.ds(r, cl), 0:_SSD_B0] * dt_w
        s_ref[d, pl.ds(r, cl), :] = _dot(bt_b, (xd * jnp.exp(tot_w - cs_w)).astype(BF16))
        xd_b = xd.astype(BF16)
        yield
        ys = []
        for j in range(SSD_HPG):
            ln = d * SSD_HPG + j
            lmat = jnp.exp(jnp.where(tris[d], cs[:, ln:ln + 1] - cs_t[ln:ln + 1, :], -jnp.inf))
            ys.append(_dot((cb * lmat).astype(BF16), xd_b[:, j * SSD_HD:(j + 1) * SSD_HD]))
            yield
        out.append(jnp.concatenate(ys, axis=1))

    def local_body(it, carry):
        rows = [pl.multiple_of((it * SSD_UNROLL + u) * cl, cl) for u in range(SSD_UNROLL)]
        outs = [[] for _ in rows]
        shared = []
        for r in rows:
            bmat = act_ref[pl.ds(r, cl), _SSD_B0:_SSD_C0]
            c_b = act_ref[pl.ds(r, cl), _SSD_C0:_SSD_W].astype(BF16)
            shared.append((_dot_nt(c_b, bmat.astype(BF16)), bmat.T.astype(BF16)))
        strands = [local_strand(d, r, sh, o) for r, sh, o in zip(rows, shared, outs) for d in range(2)]
        while strands:
            for s in list(strands):
                if next(s, "done") == "done":
                    strands.remove(s)
        for r, o in zip(rows, outs):
            y_ref[0, pl.ds(r, cl), :] = skip * act_ref[pl.ds(r, cl), 0:_SSD_B0] + o[0] + o[1]
        return carry

    lax.fori_loop(0, n_chunks // SSD_UNROLL, local_body, 0)

    h_ref[...] = jnp.zeros(h_ref.shape, F32)

    def scan_body(i, carry):
        c_fwd = jnp.where(i < n_ctx_chunks, n_lat_chunks + i, i - n_ctx_chunks)
        for d, c in ((0, c_fwd), (1, n_chunks - 1 - i)):
            r = pl.multiple_of(c * cl, cl)
            h = h_ref[d]
            c_b = act_ref[pl.ds(r, cl), _SSD_C0:_SSD_W].astype(BF16)
            y_off = _dot(c_b, h.astype(BF16)) * e_ref[d, pl.ds(r, cl), :]
            decay = e_ref[d, pl.ds(r + (0 if d else cl - 1), 1), :]
            h_ref[d] = decay * h + s_ref[d, pl.ds(r, cl), :]
            y_ref[0, pl.ds(r, cl), :] = y_ref[0, pl.ds(r, cl), :] + y_off
        return carry

    lax.fori_loop(0, n_chunks, scan_body, 0)


def _ssd(xbc, dtr, lw, *, n_lat, n_ctx):
    b, t_all, _ = xbc.shape
    gw = SSD_HPG * SSD_HD
    assert (t_all // SSD_CHUNK) % SSD_UNROLL == 0
    xs_blk = lambda i, g: (i, 0, g)
    b_blk = lambda i, g: (i, 0, SSD_INNER // SSD_STATE + g)
    c_blk = lambda i, g: (i, 0, SSD_INNER // SSD_STATE + SSD_GROUPS + g)
    return pl.pallas_call(
        functools.partial(_ssd_kernel, n_lat=n_lat, n_ctx=n_ctx),
        grid=(b, SSD_GROUPS),
        in_specs=[
            pl.BlockSpec((1, t_all, gw), xs_blk),
            pl.BlockSpec((1, t_all, SSD_STATE), b_blk),
            pl.BlockSpec((1, t_all, SSD_STATE), c_blk),
            pl.BlockSpec((1, t_all, LANE), lambda i, g: (i, 0, 0)),
            pl.BlockSpec((SSD_CONV, gw), lambda i, g: (0, g)),
            pl.BlockSpec((SSD_CONV, SSD_STATE), lambda i, g: (0, SSD_INNER // SSD_STATE + g)),
            pl.BlockSpec((SSD_CONV, SSD_STATE), lambda i, g: (0, SSD_INNER // SSD_STATE + SSD_GROUPS + g)),
            pl.BlockSpec((1, gw), lambda i, g: (0, g)),
            pl.BlockSpec((1, SSD_STATE), lambda i, g: (0, SSD_INNER // SSD_STATE + g)),
            pl.BlockSpec((1, SSD_STATE), lambda i, g: (0, SSD_INNER // SSD_STATE + SSD_GROUPS + g)),
            pl.BlockSpec((1, 8, LANE), lambda i, g: (g, 0, 0)),
            pl.BlockSpec((1, 2, gw), lambda i, g: (g, 0, 0)),
        ],
        out_specs=pl.BlockSpec((1, t_all, gw), xs_blk),
        out_shape=jax.ShapeDtypeStruct((b, t_all, SSD_INNER), F32),
        scratch_shapes=[
            pltpu.VMEM((t_all + 3 * _PAD_ROWS, _SSD_W), F32),
            pltpu.VMEM((t_all, _SSD_W), F32),
            pltpu.VMEM((t_all, LANE), F32),
            pltpu.VMEM((2, t_all, gw), F32),
            pltpu.VMEM((2, t_all, gw), F32),
            pltpu.VMEM((2, SSD_STATE, gw), F32),
        ],
        compiler_params=_cparams(2),
        name="ssd_scan",
    )(xbc, xbc, xbc, dtr, lw["conv_w"], lw["conv_w"], lw["conv_w"], lw["conv_b"], lw["conv_b"],
      lw["conv_b"], lw["ssd_par"], lw["ssd_skip"])


def _attend_heads(qs, load_ks, load_vts, k_lo, k_hi):
    n = len(qs)
    tasks = [(h, lo, min(lo + KEY_CHUNK, k_hi)) for lo in range(k_lo, k_hi, KEY_CHUNK) for h in range(n)]
    nt = len(tasks)
    m, l, acc = [None] * n, [None] * n, [None] * n
    s, shift, alpha, p = {}, {}, {}, {}
    d_max, d_exp, d_pv = ATTN_STAGES
    for i in range(nt + d_pv):
        if i < nt:
            h, lo, hi = tasks[i]
            s[i] = _dot_nt(load_ks[h](lo, hi), qs[h])
        j = i - d_max
        if 0 <= j < nt:
            h = tasks[j][0]
            cm = jnp.max(s[j], axis=0, keepdims=True)
            if m[h] is None:
                alpha[j] = None
                m[h] = cm
            else:
                m_new = jnp.maximum(m[h], cm)
                alpha[j] = jnp.exp2(m[h] - m_new)
                m[h] = m_new
            shift[j] = m[h]
        j = i - d_exp
        if 0 <= j < nt:
            p[j] = jnp.exp2(s.pop(j) - shift.pop(j)).astype(BF16)
        j = i - d_pv
        if 0 <= j < nt:
            h, lo, hi = tasks[j]
            v_ext = jnp.concatenate([load_vts[h](lo, hi), jnp.ones((SUM_ROWS, hi - lo), BF16)], axis=0)
            pv = _dot(v_ext, p.pop(j))
            a = alpha.pop(j)
            acc[h] = pv if a is None else a * acc[h] + pv
    dv = acc[0].shape[0] - SUM_ROWS
    return [acc[h][0:dv, :] / acc[h][dv:dv + 1, :] for h in range(n)]


def _by_query_tile(compute, n_lat, n_ctx, nq_lat, with_ctx_queries):
    if not with_ctx_queries:
        compute(0, n_lat + n_ctx)
        return
    qi = pl.program_id(2)

    @pl.when(qi < nq_lat)
    def _():
        compute(0, n_lat + n_ctx)

    @pl.when(qi >= nq_lat)
    def _():
        compute(n_lat, n_lat + n_ctx)


def _store_heads(o_ref, outs):
    for i in range(0, len(outs), 2):
        pair = jnp.concatenate(outs[i:i + 2], axis=0).T.astype(BF16)
        o_ref[0, :, i // 2 * LANE:(i // 2 + 1) * LANE] = pair


def _mla_attn_kernel(q_ref, k_ref, vt_ref, o_ref, *, n_lat, n_ctx, nq_lat, with_ctx_queries):
    def compute(k_lo, k_hi):
        lanes = [slice(i * HEAD_PAD, (i + 1) * HEAD_PAD) for i in range(ATTN_HEADS)]
        rows = [slice(i * MLA_V, (i + 1) * MLA_V) for i in range(ATTN_HEADS)]
        outs = _attend_heads([q_ref[0, :, ln] for ln in lanes],
                             [lambda lo, hi, ln=ln: k_ref[0, lo:hi, ln] for ln in lanes],
                             [lambda lo, hi, rw=rw: vt_ref[0, rw, lo:hi] for rw in rows], k_lo, k_hi)
        _store_heads(o_ref, outs)

    _by_query_tile(compute, n_lat, n_ctx, nq_lat, with_ctx_queries)


def _gqa_attn_kernel(q_ref, k_ref, vt_ref, o_ref, *, n_lat, n_ctx, nq_lat, with_ctx_queries):
    def compute(k_lo, k_hi):
        qs = []
        for i in range(ATTN_HEADS):
            q2 = q_ref[0, :, i // 2 * LANE:(i // 2 + 1) * LANE]
            lane = lax.broadcasted_iota(jnp.int32, q2.shape, 1)
            sel = (lane < GQA_HD) if i % 2 == 0 else (lane >= GQA_HD)
            qs.append(jnp.where(sel, q2, jnp.zeros_like(q2)))
        outs = _attend_heads(qs, [lambda lo, hi: k_ref[0, lo:hi, :]] * ATTN_HEADS,
                             [lambda lo, hi: vt_ref[0, :, lo:hi]] * ATTN_HEADS, k_lo, k_hi)
        _store_heads(o_ref, outs)

    _by_query_tile(compute, n_lat, n_ctx, nq_lat, with_ctx_queries)


def _attention(kind, q, k, v_t, *, n_lat, n_ctx, with_ctx_queries):
    b, t_all, _ = q.shape
    tq = TOKEN_TILE
    nq_lat = n_lat // tq
    nq = t_all // tq if with_ctx_queries else nq_lat
    nh = ATTN_HEADS
    assert nh == GQA_HEADS // GQA_KV_HEADS and MLA_HEADS % nh == 0
    if kind == "mla":
        body = _mla_attn_kernel
        in_specs = [
            pl.BlockSpec((1, tq, nh * HEAD_PAD), lambda i, p, j: (i, j, p)),
            pl.BlockSpec((1, t_all, nh * HEAD_PAD), lambda i, p, j: (i, 0, p)),
            pl.BlockSpec((1, nh * MLA_V, t_all), lambda i, p, j: (i, p, 0)),
        ]
    else:
        body = _gqa_attn_kernel
        in_specs = [
            pl.BlockSpec((1, tq, nh * GQA_HD), lambda i, p, j: (i, j, p)),
            pl.BlockSpec((1, t_all, 2 * GQA_HD), lambda i, p, j: (i, 0, p)),
            pl.BlockSpec((1, GQA_HD, t_all), lambda i, p, j: (i, p, 0)),
        ]
    return pl.pallas_call(
        functools.partial(body, n_lat=n_lat, n_ctx=n_ctx, nq_lat=nq_lat, with_ctx_queries=with_ctx_queries),
        grid=(b, MLA_HEADS // nh, nq),
        in_specs=in_specs,
        out_specs=pl.BlockSpec((1, tq, nh * MLA_V), lambda i, p, j: (i, j, p)),
        out_shape=jax.ShapeDtypeStruct((b, t_all, MLA_HEADS * MLA_V), BF16),
        compiler_params=_cparams(3),
        name=kind + "_attention",
    )(q, k, v_t)


def _out_kernel(x_ref, modx_ref, modc_ref, gpre_ref, gpost_ref, omla_ref, ogqa_ref, y_ref, wzg_ref, gssd_ref,
                wmo_ref, wgo_ref, wso_ref, wout_ref, o_ref, *, tm, n_lat):
    t = pl.program_id(1)
    d = x_ref.shape[2]
    gate_cols = lambda i: slice(SSD_INNER + i * d, SSD_INNER + (i + 1) * d)

    def strand(r0):
        rows = slice(r0, r0 + OUT_SUB)
        mod = lambda k: _mod_rows(t, tm, n_lat, modx_ref, modc_ref, k, r0, OUT_SUB)
        x = x_ref[0, rows, :]
        u = (_rms(x, gpre_ref[...]) * (1.0 + mod(4)) + mod(3)).astype(BF16)
        yield
        z = _dot(u, wzg_ref[:, 0:SSD_INNER])
        a_mla = _dot(omla_ref[0, rows, :], wmo_ref[...])
        g0 = _dot(u, wzg_ref[:, gate_cols(0)])
        yield
        a_gqa = _dot(ogqa_ref[0, rows, :], wgo_ref[...])
        g1 = _dot(u, wzg_ref[:, gate_cols(1)])
        y = _rms(y_ref[0, rows, :] * (z * jax.nn.sigmoid(z)), gssd_ref[...]).astype(BF16)
        yield
        g2 = _dot(u, wzg_ref[:, gate_cols(2)])
        a_ssd = _dot(y, wso_ref[...])
        merged = jax.nn.sigmoid(g0) * a_mla + jax.nn.sigmoid(g1) * a_gqa
        yield
        merged = (merged + jax.nn.sigmoid(g2) * a_ssd).astype(BF16)
        out = _dot(merged, wout_ref[...])
        yield
        o_ref[0, rows, :] = x + mod(5) * _rms(out, gpost_ref[...])

    _emit_skewed([strand(r0) for r0 in range(0, tm, OUT_SUB)])


def _mixer_out(xc, modx, modc, omla, ogqa, y, lw, *, n_lat, n_rows):
    b, t_all, d = xc.shape
    tm = OUT_SUB
    row = lambda w: pl.BlockSpec((1, tm, w), lambda i, j: (i, j, 0))
    res = [lw["wzg"], lw["g_ssd"], lw["w_mla_o"], lw["w_gqa_o"], lw["w_ssd_o"], lw["w_out"]]
    return pl.pallas_call(
        functools.partial(_out_kernel, tm=tm, n_lat=n_lat),
        grid=(b, n_rows // tm),
        in_specs=[row(d), pl.BlockSpec((1, N_MOD, d), lambda i, j: (i, 0, 0)), _resident((N_MOD, d)),
                  _resident((1, d)), _resident((1, d)),
                  row(MLA_HEADS * MLA_V), row(GQA_HEADS * GQA_HD), row(SSD_INNER)]
        + [_resident(w.shape) for w in res],
        out_specs=row(d),
        out_shape=jax.ShapeDtypeStruct((b, t_all, d), F32),
        input_output_aliases={0: 0},
        compiler_params=_cparams(2),
        name="mixer_out",
    )(xc, modx, modc, lw["g_pre1"], lw["g_post1"], omla, ogqa, y, *res)


def _rope_tables(n_lat, n_ctx):
    rows = n_lat // GRID_W
    r = jnp.repeat(jnp.arange(rows, dtype=F32), GRID_W)
    c = jnp.tile(jnp.arange(GRID_W, dtype=F32), rows)

    def angles(rot_dim):
        n_freq = rot_dim // 4
        inv = ROPE_THETA ** (-jnp.arange(n_freq, dtype=F32) / n_freq)
        return jnp.concatenate([r[:, None] * inv, c[:, None] * inv], axis=-1)

    def finish(cos_l, sin_l):
        ident_c = jnp.ones((n_ctx, LANE), F32)
        ident_s = jnp.zeros((n_ctx, LANE), F32)
        return jnp.concatenate([cos_l, ident_c], 0), jnp.concatenate([sin_l, ident_s], 0)

    ang = angles(MLA_ROPE)
    cos, sin = jnp.cos(ang), jnp.sin(ang)
    ones = jnp.ones((n_lat, MLA_NOPE), F32)
    zeros = jnp.zeros((n_lat, MLA_NOPE), F32)
    tail1 = jnp.ones((n_lat, HEAD_PAD - MLA_NOPE - MLA_ROPE), F32)
    tail0 = jnp.zeros((n_lat, HEAD_PAD - MLA_NOPE - MLA_ROPE), F32)
    mla = finish(jnp.concatenate([ones, cos, cos, tail1], 1), jnp.concatenate([zeros, -sin, sin, tail0], 1))

    ang = angles(GQA_HD)
    cos, sin = jnp.cos(ang), jnp.sin(ang)
    gqa = finish(jnp.concatenate([cos, cos, cos, cos], 1), jnp.concatenate([-sin, sin, -sin, sin], 1))
    return mla + gqa


def _head_indicator():
    head = jnp.arange(GQA_HEADS * GQA_HD) // GQA_HD
    e = (head[:, None] == jnp.arange(LANE)[None, :])
    return (e.astype(F32) / GQA_HD).astype(BF16)


def _group_lanes(v):
    return v.reshape(2, SSD_GROUPS, SSD_HPG).transpose(1, 0, 2).reshape(SSD_GROUPS, 2 * SSD_HPG)


def _layer_weights(l, p):
    d = p["w_in"].shape[1]
    w = p["w_in"][l]
    kg = w[:, _C_KG:_C_VG].reshape(d, GQA_KV_HEADS, GQA_HD)
    kdup = jnp.concatenate([kg, kg], axis=-1).reshape(d, 2 * GQA_KV_HEADS * GQA_HD)
    kr = jnp.zeros((d, HEAD_PAD), F32).at[:, MLA_NOPE:MLA_NOPE + MLA_ROPE].set(w[:, _C_KROPE:_C_KG])
    wdt = w[:, _C_DT:_C_QLAT].reshape(d, 2, SSD_GROUPS, SSD_HPG).transpose(0, 2, 1, 3).reshape(d, 2 * SSD_HEADS)
    wdt = jnp.pad(wdt, ((0, 0), (0, LANE - 2 * SSD_HEADS)))
    wa = jnp.concatenate([w[:, _C_KVLAT:_C_KROPE], kdup, w[:, _C_XBC:_C_DT], w[:, _C_QLAT:_C_QG],
                          w[:, _C_QG:_C_Z], kr, wdt], axis=1).astype(BF16)

    wkv = p["w_mla_kv_up"][l].reshape(MLA_KV_RANK, MLA_HEADS, MLA_NOPE + MLA_V)
    wk = jnp.pad(wkv[:, :, :MLA_NOPE], ((0, 0), (0, 0), (0, HEAD_PAD - MLA_NOPE)))
    wk = wk.reshape(MLA_KV_RANK, MLA_HEADS * HEAD_PAD).astype(BF16)
    wv_t = wkv[:, :, MLA_NOPE:].reshape(MLA_KV_RANK, MLA_HEADS * MLA_V).T.astype(BF16)
    wq = p["w_mla_q_up"][l].reshape(MLA_Q_RANK, MLA_HEADS, MLA_NOPE + MLA_ROPE)
    wq = jnp.pad(wq, ((0, 0), (0, 0), (0, HEAD_PAD - MLA_NOPE - MLA_ROPE)))
    wq = wq.reshape(MLA_Q_RANK, MLA_HEADS * HEAD_PAD).astype(BF16)

    par = jnp.zeros((SSD_GROUPS, 8, LANE), F32)
    par = par.at[:, 0, :2 * SSD_HPG].set(_group_lanes(p["dt_bias"][l]))
    par = par.at[:, 1, :2 * SSD_HPG].set(_group_lanes(p["a_log"][l]))
    skip = jnp.repeat(p["ssd_skip"][l].reshape(2, SSD_GROUPS, SSD_HPG), SSD_HD, axis=2).transpose(1, 0, 2)

    row = lambda v: v.reshape(1, -1).astype(F32)
    ffn = lambda s: (p["w_ffn_gate"][l, s].astype(BF16), p["w_ffn_up"][l, s].astype(BF16),
                     p["w_ffn_down"][l, s].astype(BF16))
    return {
        "ffn0": ffn(0), "ffn1": ffn(1),
        "g_pre0": row(p["g_pre"][l, 0]), "g_pre1": row(p["g_pre"][l, 1]), "g_pre2": row(p["g_pre"][l, 2]),
        "g_post0": row(p["g_post"][l, 0]), "g_post1": row(p["g_post"][l, 1]), "g_post2": row(p["g_post"][l, 2]),
        "wa": wa, "wvg_t": w[:, _C_VG:_C_XBC].T.astype(BF16),
        "g_mla_kv": row(p["g_mla_kv"][l]), "wk": wk, "wv_t": wv_t,
        "g_mla_q": row(p["g_mla_q"][l]), "wq": wq, "e": _head_indicator(),
        "g_gqa_k": row(jnp.tile(p["g_gqa_k"][l], 2 * GQA_KV_HEADS)),
        "g_gqa_q": row(jnp.tile(p["g_gqa_q"][l], GQA_HEADS)),
        "conv_w": p["conv_w"][l], "conv_b": row(p["conv_b"][l]), "ssd_par": par, "ssd_skip": skip,
        "wzg": w[:, _C_Z:].astype(BF16), "g_ssd": row(p["g_ssd"][l]),
        "w_mla_o": p["w_mla_o"][l].astype(BF16), "w_gqa_o": p["w_gqa_o"][l].astype(BF16),
        "w_ssd_o": p["w_ssd_o"][l].astype(BF16), "w_out": p["w_out"][l].astype(BF16),
    }


def kernel(x, c, ctx, c_ctx, w_mod, b_mod, g_pre, g_post, w_ffn_gate, w_ffn_up, w_ffn_down, w_in, g_mla_q, w_mla_q_up, g_mla_kv, w_mla_kv_up, g_gqa_q, g_gqa_k, conv_w, conv_b, dt_bias, a_log, ssd_skip, g_ssd, w_mla_o, w_gqa_o, w_ssd_o, w_out):
    b, n_lat, d = x.shape
    n_ctx = ctx.shape[1]
    t_all = n_lat + n_ctx
    depth = w_in.shape[0]
    assert n_lat % TOKEN_TILE == 0 and n_ctx % TOKEN_TILE == 0 and n_lat % GRID_W == 0
    assert w_in.shape[2] == _C_GATE + 3 * d and d == SSD_INNER
    p = dict(g_pre=g_pre, g_post=g_post, w_ffn_gate=w_ffn_gate, w_ffn_up=w_ffn_up, w_ffn_down=w_ffn_down,
             w_in=w_in, g_mla_q=g_mla_q, w_mla_q_up=w_mla_q_up, g_mla_kv=g_mla_kv, w_mla_kv_up=w_mla_kv_up,
             g_gqa_q=g_gqa_q, g_gqa_k=g_gqa_k, conv_w=conv_w, conv_b=conv_b, dt_bias=dt_bias, a_log=a_log,
             ssd_skip=ssd_skip, g_ssd=g_ssd, w_mla_o=w_mla_o, w_gqa_o=w_gqa_o, w_ssd_o=w_ssd_o, w_out=w_out)

    rows = -(-(b + 1) // 8) * 8
    c_all = jnp.concatenate([c, c_ctx[None, :], jnp.zeros((rows - b - 1, d), F32)], axis=0)
    mod = _modulation(c_all, w_mod, b_mod).reshape(depth, rows, N_MOD, d)
    tabs = _rope_tables(n_lat, n_ctx)
    xc = jnp.concatenate([x, ctx], axis=1)

    for l in range(depth):
        last = l == depth - 1
        lw = _layer_weights(l, p)
        modx, modc = mod[l, :b], mod[l, b]
        xc = _ffn(xc, modx, modc, lw["g_pre0"], lw["g_post0"], *lw["ffn0"],
                  k0=0, n_lat=n_lat, n_rows=t_all, alias=True)
        kmla, vtmla, qmla, kgqa, vtgqa, qgqa, xbc, dtr = _in_proj(xc, modx, modc, lw, tabs, n_lat=n_lat)
        y = _ssd(xbc, dtr, lw, n_lat=n_lat, n_ctx=n_ctx)
        omla = _attention("mla", qmla, kmla, vtmla, n_lat=n_lat, n_ctx=n_ctx, with_ctx_queries=not last)
        ogqa = _attention("gqa", qgqa, kgqa, vtgqa, n_lat=n_lat, n_ctx=n_ctx, with_ctx_queries=not last)
        n_rows = n_lat if last else t_all
        xc = _mixer_out(xc, modx, modc, omla, ogqa, y, lw, n_lat=n_lat, n_rows=n_rows)
        xc = _ffn(xc, modx, modc, lw["g_pre2"], lw["g_post2"], *lw["ffn1"],
                  k0=6, n_lat=n_lat, n_rows=n_rows, alias=not last)
    return xc
```

```python
import functools
import math

import jax
import jax.numpy as jnp
from jax import lax
from jax.experimental import pallas as pl
from jax.experimental.pallas import tpu as pltpu

F32 = jnp.float32
BF16 = jnp.bfloat16

GRID_W = 64
ROPE_THETA = 10000.0
NORM_EPS = 1e-6
FFN_RES = 0.5
N_MOD = 9

MLA_HEADS = 16
MLA_NOPE = 64
MLA_ROPE = 32
MLA_V = 64
MLA_Q_RANK = 512
MLA_KV_RANK = 256
GQA_HEADS = 16
GQA_KV_HEADS = 4
GQA_HD = 64
SSD_INNER = 1024
SSD_HD = 64
SSD_HEADS = 16
SSD_GROUPS = 4
SSD_HPG = 4
SSD_STATE = 128
SSD_CONV = 5
SSD_CHUNK = 128
SSD_CONV_DIM = SSD_INNER + 2 * SSD_GROUPS * SSD_STATE
MLA_SCALE = (MLA_NOPE + MLA_ROPE) ** -0.5
GQA_SCALE = GQA_HD ** -0.5
LOG2E = math.log2(math.e)

LANE = 128
HEAD_PAD = 128
TOKEN_TILE = 256
SUB_TILE = 256
OUT_SUB = 256
FFN_COLS = 768
KEY_CHUNK = 256
ATTN_HEADS = 16
SUM_ROWS = 16
ATTN_STAGES = (4, 8, 12)
VMEM_LIMIT = 56 * 1024 * 1024

_C_KVLAT = 0
_C_KROPE = _C_KVLAT + MLA_KV_RANK
_C_KG = _C_KROPE + MLA_ROPE
_C_VG = _C_KG + GQA_KV_HEADS * GQA_HD
_C_XBC = _C_VG + GQA_KV_HEADS * GQA_HD
_C_DT = _C_XBC + SSD_CONV_DIM
_C_QLAT = _C_DT + 2 * SSD_HEADS
_C_QG = _C_QLAT + MLA_Q_RANK
_C_Z = _C_QG + GQA_HEADS * GQA_HD
_C_GATE = _C_Z + SSD_INNER

_A_KVLAT = 0
_A_KDUP = _A_KVLAT + MLA_KV_RANK
_A_XBC = _A_KDUP + 2 * GQA_KV_HEADS * GQA_HD
_A_QLAT = _A_XBC + SSD_CONV_DIM
_A_QG = _A_QLAT + MLA_Q_RANK
_A_KR = _A_QG + GQA_HEADS * GQA_HD
_A_DT = _A_KR + HEAD_PAD
_A_END = _A_DT + LANE


def _cparams(n_axes):
    return pltpu.CompilerParams(
        dimension_semantics=("parallel",) * n_axes, vmem_limit_bytes=VMEM_LIMIT)


def _resident(shape):
    nd = len(shape)
    return pl.BlockSpec(shape, lambda *_: (0,) * nd, pipeline_mode=pl.Buffered(1))


def _rms(t, g):
    return t * lax.rsqrt(jnp.mean(t * t, axis=-1, keepdims=True) + NORM_EPS) * g


def _dot(a, b):
    return jnp.dot(a, b, preferred_element_type=F32)


def _dot_nt(a, b):
    return lax.dot_general(a, b, (((1,), (1,)), ((), ())), preferred_element_type=F32)


def _mod_rows(tile_idx, tm, n_lat, modx_ref, modc_ref, k, row0=0, n_rows=None):
    n_rows = tm if n_rows is None else n_rows
    rows = tile_idx * tm + row0 + lax.broadcasted_iota(jnp.int32, (n_rows, 1), 0)
    return jnp.where(rows >= n_lat, modc_ref[k:k + 1, :], modx_ref[0, k:k + 1, :])


def _emit_skewed(strands):
    pending, active = list(strands), []
    while pending or active:
        if pending:
            active.append(pending.pop(0))
        for s in list(active):
            if next(s, "done") == "done":
                active.remove(s)


def _mod_kernel(c_ref, w_ref, b_ref, o_ref):
    s = c_ref[...]
    s = s * jax.nn.sigmoid(s)
    o_ref[0] = jnp.dot(s, w_ref[0], preferred_element_type=F32,
                       precision=lax.Precision.HIGHEST) + b_ref[0]


def _modulation(c_all, w_mod, b_mod):
    n_layers, d, _ = w_mod.shape
    rows = c_all.shape[0]
    return pl.pallas_call(
        _mod_kernel,
        grid=(n_layers, N_MOD),
        in_specs=[
            pl.BlockSpec((rows, d), lambda l, j: (0, 0)),
            pl.BlockSpec((1, d, d), lambda l, j: (l, 0, j)),
            pl.BlockSpec((1, 1, d), lambda l, j: (l, 0, j)),
        ],
        out_specs=pl.BlockSpec((1, rows, d), lambda l, j: (l, 0, j)),
        out_shape=jax.ShapeDtypeStruct((n_layers, rows, N_MOD * d), F32),
        compiler_params=_cparams(2),
        name="modulation",
    )(c_all, w_mod, b_mod.reshape(n_layers, 1, N_MOD * d))


def _ffn_kernel(x_ref, modx_ref, modc_ref, gin_ref, gout_ref, wg_ref, wu_ref, wd_ref, o_ref,
                *, tm, n_lat, k0):
    t = pl.program_id(1)
    f = wg_ref.shape[1]
    cuts = list(range(0, f, FFN_COLS)) + [f]

    def strand(r0):
        rows = slice(r0, r0 + SUB_TILE)
        mod = lambda k: _mod_rows(t, tm, n_lat, modx_ref, modc_ref, k, r0, SUB_TILE)
        x = x_ref[0, rows, :]
        u = (_rms(x, gin_ref[...]) * (1.0 + mod(k0 + 1)) + mod(k0)).astype(BF16)
        yield
        hidden = []
        for lo, hi in zip(cuts[:-1], cuts[1:]):
            hidden.append((_dot(u, wg_ref[:, lo:hi]), _dot(u, wu_ref[:, lo:hi])))
            yield
        y = None
        for (hg, hu), lo, hi in zip(hidden, cuts[:-1], cuts[1:]):
            a = (hg * jax.nn.sigmoid(hg) * hu).astype(BF16)
            part = _dot(a, wd_ref[lo:hi, :])
            y = part if y is None else y + part
            yield
        o_ref[0, rows, :] = x + (FFN_RES * mod(k0 + 2)) * _rms(y, gout_ref[...])

    _emit_skewed([strand(r0) for r0 in range(0, tm, SUB_TILE)])


def _row_tile(n_rows):
    for tm in (3 * SUB_TILE, 2 * SUB_TILE):
        if n_rows % tm == 0:
            return tm
    return SUB_TILE


def _ffn(xc, modx, modc, g_in, g_out, wg, wu, wd, *, k0, n_lat, n_rows, alias):
    b, t_all, d = xc.shape
    f = wg.shape[1]
    tm = _row_tile(n_rows)
    out_rows = t_all if alias else n_rows
    return pl.pallas_call(
        functools.partial(_ffn_kernel, tm=tm, n_lat=n_lat, k0=k0),
        grid=(b, n_rows // tm),
        in_specs=[
            pl.BlockSpec((1, tm, d), lambda i, j: (i, j, 0)),
            pl.BlockSpec((1, N_MOD, d), lambda i, j: (i, 0, 0)),
            _resident((N_MOD, d)),
            _resident((1, d)),
            _resident((1, d)),
            _resident((d, f)),
            _resident((d, f)),
            _resident((f, d)),
        ],
        out_specs=pl.BlockSpec((1, tm, d), lambda i, j: (i, j, 0)),
        out_shape=jax.ShapeDtypeStruct((b, out_rows, d), F32),
        input_output_aliases={0: 0} if alias else {},
        compiler_params=_cparams(2),
        name="ffn_half_step",
    )(xc, modx, modc, g_in, g_out, wg, wu, wd)


def _tile_lanes(t, reps):
    return t if reps == 1 else jnp.concatenate([t] * reps, axis=1)


def _rope_gqa(x, cos_t, sin_t):
    w = x.shape[1]
    reps = w // LANE
    lane = lax.broadcasted_iota(jnp.int32, x.shape, 1) % GQA_HD
    half = GQA_HD // 2
    partner = jnp.where(lane < half, pltpu.roll(x, w - half, 1), pltpu.roll(x, half, 1))
    return x * _tile_lanes(cos_t, reps) + partner * _tile_lanes(sin_t, reps)


def _rope_mla(x, cos_t, sin_t):
    w = x.shape[1]
    reps = w // LANE
    lane = lax.broadcasted_iota(jnp.int32, x.shape, 1) % HEAD_PAD
    half = MLA_ROPE // 2
    first = jnp.logical_and(lane >= MLA_NOPE, lane < MLA_NOPE + half)
    partner = jnp.where(first, pltpu.roll(x, w - half, 1), pltpu.roll(x, half, 1))
    return x * _tile_lanes(cos_t, reps) + partner * _tile_lanes(sin_t, reps)


def _head_scale(x, ms, gain):
    r = lax.rsqrt(ms + NORM_EPS)
    lane = lax.broadcasted_iota(jnp.int32, r.shape, 1)
    second = (lane >= GQA_HD).astype(jnp.int32)
    tiles = [jnp.take_along_axis(r, 2 * j + second, axis=1) for j in range(x.shape[1] // LANE)]
    return x * jnp.concatenate(tiles, axis=1) * gain


def _in_proj_kernel(x_ref, modx_ref, modc_ref, gpre_ref, wa_ref, wvg_t_ref, gkv_ref, wk_ref, wv_t_ref,
                    gq_ref, wq_ref, e_ref, ggk_ref, ggq_ref,
                    mla_cos_ref, mla_sin_ref, gqa_cos_ref, gqa_sin_ref,
                    kmla_ref, vtmla_ref, qmla_ref, kgqa_ref, vtgqa_ref, qgqa_ref, xbc_ref, dt_ref,
                    *, tm, n_lat):
    t = pl.program_id(1)
    x = x_ref[0]
    shift = _mod_rows(t, tm, n_lat, modx_ref, modc_ref, 3)
    scale = _mod_rows(t, tm, n_lat, modx_ref, modc_ref, 4)
    u = (_rms(x, gpre_ref[...]) * (1.0 + scale) + shift).astype(BF16)
    proj = lambda lo, hi: _dot(u, wa_ref[:, lo:hi])
    mla_cos = mla_cos_ref[...]
    mla_sin = mla_sin_ref[...]
    gqa_cos = gqa_cos_ref[...]
    gqa_sin = gqa_sin_ref[...]
    n_k = 2 * GQA_KV_HEADS * GQA_HD

    h_kv = proj(_A_KVLAT, _A_KDUP)
    h_q = proj(_A_QLAT, _A_QG)
    h_kd = proj(_A_KDUP, _A_XBC)
    h_qg = proj(_A_QG, _A_KR)
    h_kr = proj(_A_KR, _A_END)
    ckv = _rms(h_kv, gkv_ref[...]).astype(BF16)
    cq = _rms(h_q, gq_ref[...]).astype(BF16)
    k_nope = _dot(ckv, wk_ref[...])
    v_t = _dot_nt(wv_t_ref[...], ckv)
    q_mla = _dot(cq, wq_ref[...])
    ms_k = _dot((h_kd * h_kd).astype(BF16), e_ref[0:n_k, :])
    ms_q = _dot((h_qg * h_qg).astype(BF16), e_ref[...])
    vg_t = _dot_nt(wvg_t_ref[...], u)

    dt_ref[0] = h_kr[:, HEAD_PAD:]
    k_rope = _rope_mla(h_kr[:, 0:HEAD_PAD], mla_cos, mla_sin)
    kmla_ref[0] = (k_nope + _tile_lanes(k_rope, MLA_HEADS)).astype(BF16)
    vtmla_ref[0] = v_t.astype(BF16)
    qmla_ref[0] = (_rope_mla(q_mla, mla_cos, mla_sin) * (MLA_SCALE * LOG2E)).astype(BF16)
    kd = _head_scale(h_kd, ms_k, ggk_ref[...])
    kgqa_ref[0] = _rope_gqa(kd, gqa_cos, gqa_sin).astype(BF16)
    qg = _head_scale(h_qg, ms_q, ggq_ref[...])
    qgqa_ref[0] = (_rope_gqa(qg, gqa_cos, gqa_sin) * (GQA_SCALE * LOG2E)).astype(BF16)
    vtgqa_ref[0] = vg_t.astype(BF16)
    xbc_ref[0] = proj(_A_XBC, _A_QLAT)


def _in_proj(xc, modx, modc, lw, tabs, *, n_lat):
    b, t_all, d = xc.shape
    tm = TOKEN_TILE
    n_qm = MLA_HEADS * HEAD_PAD
    n_kd = 2 * GQA_KV_HEADS * GQA_HD
    n_qg = GQA_HEADS * GQA_HD
    n_vm = MLA_HEADS * MLA_V
    n_vg = GQA_KV_HEADS * GQA_HD
    row = lambda w: pl.BlockSpec((1, tm, w), lambda i, j: (i, j, 0))
    col = lambda h: pl.BlockSpec((1, h, tm), lambda i, j: (i, 0, j))
    tab = pl.BlockSpec((tm, LANE), lambda i, j: (j, 0))
    res = [lw["g_pre1"], lw["wa"], lw["wvg_t"], lw["g_mla_kv"], lw["wk"], lw["wv_t"],
           lw["g_mla_q"], lw["wq"], lw["e"], lw["g_gqa_k"], lw["g_gqa_q"]]
    return pl.pallas_call(
        functools.partial(_in_proj_kernel, tm=tm, n_lat=n_lat),
        grid=(b, t_all // tm),
        in_specs=[row(d), pl.BlockSpec((1, N_MOD, d), lambda i, j: (i, 0, 0)), _resident((N_MOD, d))]
        + [_resident(w.shape) for w in res] + [tab] * 4,
        out_specs=[row(n_qm), col(n_vm), row(n_qm), row(n_kd), col(n_vg), row(n_qg),
                   row(SSD_CONV_DIM), row(LANE)],
        out_shape=[
            jax.ShapeDtypeStruct((b, t_all, n_qm), BF16),
            jax.ShapeDtypeStruct((b, n_vm, t_all), BF16),
            jax.ShapeDtypeStruct((b, t_all, n_qm), BF16),
            jax.ShapeDtypeStruct((b, t_all, n_kd), BF16),
            jax.ShapeDtypeStruct((b, n_vg, t_all), BF16),
            jax.ShapeDtypeStruct((b, t_all, n_qg), BF16),
            jax.ShapeDtypeStruct((b, t_all, SSD_CONV_DIM), F32),
            jax.ShapeDtypeStruct((b, t_all, LANE), F32),
        ],
        compiler_params=_cparams(2),
        name="in_proj",
    )(xc, modx, modc, *res, *tabs)


_SSD_W = SSD_HPG * SSD_HD + 2 * SSD_STATE
_SSD_B0 = SSD_HPG * SSD_HD
_SSD_C0 = _SSD_B0 + SSD_STATE
_PAD_ROWS = 8
SSD_UNROLL = 3


def _softplus(v):
    return jnp.maximum(v, 0.0) + jnp.log1p(jnp.exp(-jnp.abs(v)))


def _ssd_kernel(xs_ref, bm_ref, cm_ref, dt_ref, wx_ref, wb_ref, wc_ref, bx_ref, bb_ref, bc_ref,
                par_ref, skip_ref, y_ref, pad_ref, act_ref, dtv_ref, e_ref, s_ref, h_ref, *, n_lat, n_ctx):
    cl = SSD_CHUNK
    t_all = n_lat + n_ctx
    n_chunks = t_all // cl
    n_ctx_chunks = n_ctx // cl
    n_lat_chunks = n_lat // cl
    g = pl.program_id(1)

    def pad_row(r):
        return pl.multiple_of(r + jnp.where(r >= n_lat, 2 * _PAD_ROWS, _PAD_ROWS), _PAD_ROWS)

    zeros = jnp.zeros((_PAD_ROWS, _SSD_W), F32)
    pad_ref[0:_PAD_ROWS, :] = zeros
    pad_ref[_PAD_ROWS + n_lat:2 * _PAD_ROWS + n_lat, :] = zeros
    pad_ref[2 * _PAD_ROWS + t_all:3 * _PAD_ROWS + t_all, :] = zeros

    def copy_body(i, carry):
        r = pl.multiple_of(i * cl, cl)
        dst = pad_row(r)
        pad_ref[pl.ds(dst, cl), 0:_SSD_B0] = xs_ref[0, pl.ds(r, cl), :]
        pad_ref[pl.ds(dst, cl), _SSD_B0:_SSD_C0] = bm_ref[0, pl.ds(r, cl), :]
        pad_ref[pl.ds(dst, cl), _SSD_C0:_SSD_W] = cm_ref[0, pl.ds(r, cl), :]
        return carry

    lax.fori_loop(0, n_chunks, copy_body, 0)

    conv_w = jnp.concatenate([wx_ref[...], wb_ref[...], wc_ref[...]], axis=1)
    conv_b = jnp.concatenate([bx_ref[...], bb_ref[...], bc_ref[...]], axis=1)
    par = par_ref[0]
    dt_bias = par[0:1, :]
    a_rate = -jnp.exp(par[1:2, :])
    lane_shift = lax.rem(LANE - 2 * SSD_HPG * g, LANE)
    half_win = SSD_CONV // 2

    def act_body(i, carry):
        r = pl.multiple_of(i * cl, cl)
        base = pad_row(r)
        blk = pad_ref[pl.ds(base - _PAD_ROWS, cl + 2 * _PAD_ROWS), :]
        acc = conv_b
        for k in range(SSD_CONV):
            off = _PAD_ROWS + k - half_win
            acc = acc + conv_w[k:k + 1, :] * blk[off:off + cl, :]
        act_ref[pl.ds(r, cl), :] = acc * jax.nn.sigmoid(acc)
        dtv = _softplus(pltpu.roll(dt_ref[0, pl.ds(r, cl), :], lane_shift, 1) + dt_bias)
        dtv_ref[pl.ds(r, cl), :] = dtv
        return carry

    lax.fori_loop(0, n_chunks, act_body, 0)

    gw = SSD_HPG * SSD_HD
    ri = lax.broadcasted_iota(jnp.int32, (cl, cl), 0)
    ci = lax.broadcasted_iota(jnp.int32, (cl, cl), 1)
    tris = [ri >= ci, ri <= ci]
    tri_bs = [jnp.where(t, 1.0, 0.0).astype(BF16) for t in tris]
    sl = lax.broadcasted_iota(jnp.int32, (LANE, gw), 0)
    sc = lax.broadcasted_iota(jnp.int32, (LANE, gw), 1) // SSD_HD
    sels = [jnp.where(sl == sc + d * SSD_HPG, 1.0, 0.0).astype(BF16) for d in range(2)]
    skip = skip_ref[0, 0:1, :] + skip_ref[0, 1:2, :]

    def split2(v):
        hi = v.astype(BF16)
        return hi, (v - hi.astype(F32)).astype(BF16)

    def local_strand(d, r, shared, out):
        dtc = dtv_ref[pl.ds(r, cl), :]
        a_hi, a_lo = split2(dtc * a_rate)
        cs = _dot(tri_bs[d], a_hi) + _dot(tri_bs[d], a_lo)
        yield
        cs_t = cs.T
        c_hi, c_lo = split2(cs)
        dt_hi, dt_lo = split2(dtc)
        cs_w = _dot(c_hi, sels[d]) + _dot(c_lo, sels[d])
        dt_w = _dot(dt_hi, sels[d]) + _dot(dt_lo, sels[d])
        yield
        cb, bt_b = shared
        tot_w = cs_w[0:1, :] if d else cs_w[cl - 1:cl, :]
        e_ref[d, pl.ds(r, cl), :] = jnp.exp(cs_w)
        xd = act_ref[pl.ds(r, cl), 0:_SSD_B0] * dt_w
        s_ref[d, pl.ds(r, cl), :] = _dot(bt_b, (xd * jnp.exp(tot_w - cs_w)).astype(BF16))
        xd_b = xd.astype(BF16)
        yield
        first_half = lax.broadcasted_iota(jnp.int32, (cl, LANE), 1) < SSD_HD
        tiles = []
        for pair in range(SSD_HPG // 2):
            prods = []
            for j in (2 * pair, 2 * pair + 1):
                ln = d * SSD_HPG + j
                lmat = jnp.exp(jnp.where(tris[d], cs[:, ln:ln + 1] - cs_t[ln:ln + 1, :], -jnp.inf))
                prods.append(_dot((cb * lmat).astype(BF16), xd_b[:, pair * LANE:(pair + 1) * LANE]))
                yield
            tiles.append(jnp.where(first_half, prods[0], prods[1]))
        out.append(jnp.concatenate(tiles, axis=1))

    def local_body(it, carry):
        rows = [pl.multiple_of((it * SSD_UNROLL + u) * cl, cl) for u in range(SSD_UNROLL)]
        outs = [[] for _ in rows]
        shared = []
        for r in rows:
            bmat = act_ref[pl.ds(r, cl), _SSD_B0:_SSD_C0]
            c_b = act_ref[pl.ds(r, cl), _SSD_C0:_SSD_W].astype(BF16)
            shared.append((_dot_nt(c_b, bmat.astype(BF16)), bmat.T.astype(BF16)))
        strands = [local_strand(d, r, sh, o) for r, sh, o in zip(rows, shared, outs) for d in range(2)]
        while strands:
            for s in list(strands):
                if next(s, "done") == "done":
                    strands.remove(s)
        for r, o in zip(rows, outs):
            y_ref[0, pl.ds(r, cl), :] = skip * act_ref[pl.ds(r, cl), 0:_SSD_B0] + o[0] + o[1]
        return carry

    lax.fori_loop(0, n_chunks // SSD_UNROLL, local_body, 0)

    h_ref[...] = jnp.zeros(h_ref.shape, F32)

    def scan_body(it, carry):
        hs = [h_ref[0], h_ref[1]]
        todo = []
        for u in range(SSD_UNROLL):
            i = it * SSD_UNROLL + u
            c_fwd = jnp.where(i < n_ctx_chunks, n_lat_chunks + i, i - n_ctx_chunks)
            for d, c in ((0, c_fwd), (1, n_chunks - 1 - i)):
                r = pl.multiple_of(c * cl, cl)
                c_b = act_ref[pl.ds(r, cl), _SSD_C0:_SSD_W].astype(BF16)
                y_off = _dot(c_b, hs[d].astype(BF16)) * e_ref[d, pl.ds(r, cl), :]
                decay = e_ref[d, pl.ds(r + (0 if d else cl - 1), 1), :]
                hs[d] = decay * hs[d] + s_ref[d, pl.ds(r, cl), :]
                todo.append((r, y_off))
        h_ref[0] = hs[0]
        h_ref[1] = hs[1]
        for r, y_off in todo:
            y_ref[0, pl.ds(r, cl), :] = y_ref[0, pl.ds(r, cl), :] + y_off
        return carry

    lax.fori_loop(0, n_chunks // SSD_UNROLL, scan_body, 0)


def _ssd(xbc, dtr, lw, *, n_lat, n_ctx):
    b, t_all, _ = xbc.shape
    gw = SSD_HPG * SSD_HD
    assert (t_all // SSD_CHUNK) % SSD_UNROLL == 0
    xs_blk = lambda i, g: (i, 0, g)
    b_blk = lambda i, g: (i, 0, SSD_INNER // SSD_STATE + g)
    c_blk = lambda i, g: (i, 0, SSD_INNER // SSD_STATE + SSD_GROUPS + g)
    return pl.pallas_call(
        functools.partial(_ssd_kernel, n_lat=n_lat, n_ctx=n_ctx),
        grid=(b, SSD_GROUPS),
        in_specs=[
            pl.BlockSpec((1, t_all, gw), xs_blk),
            pl.BlockSpec((1, t_all, SSD_STATE), b_blk),
            pl.BlockSpec((1, t_all, SSD_STATE), c_blk),
            pl.BlockSpec((1, t_all, LANE), lambda i, g: (i, 0, 0)),
            pl.BlockSpec((SSD_CONV, gw), lambda i, g: (0, g)),
            pl.BlockSpec((SSD_CONV, SSD_STATE), lambda i, g: (0, SSD_INNER // SSD_STATE + g)),
            pl.BlockSpec((SSD_CONV, SSD_STATE), lambda i, g: (0, SSD_INNER // SSD_STATE + SSD_GROUPS + g)),
            pl.BlockSpec((1, gw), lambda i, g: (0, g)),
            pl.BlockSpec((1, SSD_STATE), lambda i, g: (0, SSD_INNER // SSD_STATE + g)),
            pl.BlockSpec((1, SSD_STATE), lambda i, g: (0, SSD_INNER // SSD_STATE + SSD_GROUPS + g)),
            pl.BlockSpec((1, 8, LANE), lambda i, g: (g, 0, 0)),
            pl.BlockSpec((1, 2, gw), lambda i, g: (g, 0, 0)),
        ],
        out_specs=pl.BlockSpec((1, t_all, gw), xs_blk),
        out_shape=jax.ShapeDtypeStruct((b, t_all, SSD_INNER), F32),
        scratch_shapes=[
            pltpu.VMEM((t_all + 3 * _PAD_ROWS, _SSD_W), F32),
            pltpu.VMEM((t_all, _SSD_W), F32),
            pltpu.VMEM((t_all, LANE), F32),
            pltpu.VMEM((2, t_all, gw), F32),
            pltpu.VMEM((2, t_all, gw), F32),
            pltpu.VMEM((2, SSD_STATE, gw), F32),
        ],
        compiler_params=_cparams(2),
        name="ssd_scan",
    )(xbc, xbc, xbc, dtr, lw["conv_w"], lw["conv_w"], lw["conv_w"], lw["conv_b"], lw["conv_b"],
      lw["conv_b"], lw["ssd_par"], lw["ssd_skip"])


def _attend_heads(qs, load_ks, load_vts, k_lo, k_hi):
    n = len(qs)
    tasks = [(h, lo, min(lo + KEY_CHUNK, k_hi)) for lo in range(k_lo, k_hi, KEY_CHUNK) for h in range(n)]
    nt = len(tasks)
    m, l, acc = [None] * n, [None] * n, [None] * n
    s, shift, alpha, p = {}, {}, {}, {}
    d_max, d_exp, d_pv = ATTN_STAGES
    for i in range(nt + d_pv):
        if i < nt:
            h, lo, hi = tasks[i]
            s[i] = _dot_nt(load_ks[h](lo, hi), qs[h])
        j = i - d_max
        if 0 <= j < nt:
            h = tasks[j][0]
            cm = jnp.max(s[j], axis=0, keepdims=True)
            if m[h] is None:
                alpha[j] = None
                m[h] = cm
            else:
                m_new = jnp.maximum(m[h], cm)
                alpha[j] = jnp.exp2(m[h] - m_new)
                m[h] = m_new
            shift[j] = m[h]
        j = i - d_exp
        if 0 <= j < nt:
            p[j] = jnp.exp2(s.pop(j) - shift.pop(j)).astype(BF16)
        j = i - d_pv
        if 0 <= j < nt:
            h, lo, hi = tasks[j]
            v_ext = jnp.concatenate([load_vts[h](lo, hi), jnp.ones((SUM_ROWS, hi - lo), BF16)], axis=0)
            pv = _dot(v_ext, p.pop(j))
            a = alpha.pop(j)
            acc[h] = pv if a is None else a * acc[h] + pv
    dv = acc[0].shape[0] - SUM_ROWS
    return [acc[h][0:dv, :] / acc[h][dv:dv + 1, :] for h in range(n)]


def _by_query_tile(compute, n_lat, n_ctx, nq_lat, with_ctx_queries):
    if not with_ctx_queries:
        compute(0, n_lat + n_ctx)
        return
    qi = pl.program_id(2)

    @pl.when(qi < nq_lat)
    def _():
        compute(0, n_lat + n_ctx)

    @pl.when(qi >= nq_lat)
    def _():
        compute(n_lat, n_lat + n_ctx)


def _store_heads(o_ref, outs):
    for i in range(0, len(outs), 2):
        pair = jnp.concatenate(outs[i:i + 2], axis=0).T.astype(BF16)
        o_ref[0, :, i // 2 * LANE:(i // 2 + 1) * LANE] = pair


def _mla_attn_kernel(q_ref, k_ref, vt_ref, o_ref, *, n_lat, n_ctx, nq_lat, with_ctx_queries):
    def compute(k_lo, k_hi):
        lanes = [slice(i * HEAD_PAD, (i + 1) * HEAD_PAD) for i in range(ATTN_HEADS)]
        rows = [slice(i * MLA_V, (i + 1) * MLA_V) for i in range(ATTN_HEADS)]
        outs = _attend_heads([q_ref[0, :, ln] for ln in lanes],
                             [lambda lo, hi, ln=ln: k_ref[0, lo:hi, ln] for ln in lanes],
                             [lambda lo, hi, rw=rw: vt_ref[0, rw, lo:hi] for rw in rows], k_lo, k_hi)
        _store_heads(o_ref, outs)

    _by_query_tile(compute, n_lat, n_ctx, nq_lat, with_ctx_queries)


def _gqa_attn_kernel(q_ref, k_ref, vt_ref, o_ref, *, n_lat, n_ctx, nq_lat, with_ctx_queries):
    def compute(k_lo, k_hi):
        qs = []
        for i in range(ATTN_HEADS):
            q2 = q_ref[0, :, i // 2 * LANE:(i // 2 + 1) * LANE]
            lane = lax.broadcasted_iota(jnp.int32, q2.shape, 1)
            sel = (lane < GQA_HD) if i % 2 == 0 else (lane >= GQA_HD)
            qs.append(jnp.where(sel, q2, jnp.zeros_like(q2)))
        rep = GQA_HEADS // GQA_KV_HEADS
        k_lanes = [slice(i // rep * LANE, (i // rep + 1) * LANE) for i in range(ATTN_HEADS)]
        v_rows = [slice(i // rep * GQA_HD, (i // rep + 1) * GQA_HD) for i in range(ATTN_HEADS)]
        outs = _attend_heads(qs, [lambda lo, hi, ln=ln: k_ref[0, lo:hi, ln] for ln in k_lanes],
                             [lambda lo, hi, rw=rw: vt_ref[0, rw, lo:hi] for rw in v_rows], k_lo, k_hi)
        _store_heads(o_ref, outs)

    _by_query_tile(compute, n_lat, n_ctx, nq_lat, with_ctx_queries)


def _attention(kind, q, k, v_t, *, n_lat, n_ctx, with_ctx_queries):
    b, t_all, _ = q.shape
    tq = TOKEN_TILE
    nq_lat = n_lat // tq
    nq = t_all // tq if with_ctx_queries else nq_lat
    nh = ATTN_HEADS
    rep = GQA_HEADS // GQA_KV_HEADS
    assert nh % rep == 0 and MLA_HEADS % nh == 0
    if kind == "mla":
        body = _mla_attn_kernel
        in_specs = [
            pl.BlockSpec((1, tq, nh * HEAD_PAD), lambda i, p, j: (i, j, p)),
            pl.BlockSpec((1, t_all, nh * HEAD_PAD), lambda i, p, j: (i, 0, p)),
            pl.BlockSpec((1, nh * MLA_V, t_all), lambda i, p, j: (i, p, 0)),
        ]
    else:
        body = _gqa_attn_kernel
        in_specs = [
            pl.BlockSpec((1, tq, nh * GQA_HD), lambda i, p, j: (i, j, p)),
            pl.BlockSpec((1, t_all, nh // rep * 2 * GQA_HD), lambda i, p, j: (i, 0, p)),
            pl.BlockSpec((1, nh // rep * GQA_HD, t_all), lambda i, p, j: (i, p, 0)),
        ]
    return pl.pallas_call(
        functools.partial(body, n_lat=n_lat, n_ctx=n_ctx, nq_lat=nq_lat, with_ctx_queries=with_ctx_queries),
        grid=(b, MLA_HEADS // nh, nq),
        in_specs=in_specs,
        out_specs=pl.BlockSpec((1, tq, nh * MLA_V), lambda i, p, j: (i, j, p)),
        out_shape=jax.ShapeDtypeStruct((b, t_all, MLA_HEADS * MLA_V), BF16),
        compiler_params=_cparams(3),
        name=kind + "_attention",
    )(q, k, v_t)


def _out_kernel(x_ref, modx_ref, modc_ref, gpre_ref, gpost_ref, omla_ref, ogqa_ref, y_ref, wzg_ref, gssd_ref,
                wmo_ref, wgo_ref, wso_ref, wout_ref, o_ref, *, tm, n_lat):
    t = pl.program_id(1)
    d = x_ref.shape[2]
    gate_cols = lambda i: slice(SSD_INNER + i * d, SSD_INNER + (i + 1) * d)

    def strand(r0):
        rows = slice(r0, r0 + OUT_SUB)
        mod = lambda k: _mod_rows(t, tm, n_lat, modx_ref, modc_ref, k, r0, OUT_SUB)
        x = x_ref[0, rows, :]
        u = (_rms(x, gpre_ref[...]) * (1.0 + mod(4)) + mod(3)).astype(BF16)
        yield
        z = _dot(u, wzg_ref[:, 0:SSD_INNER])
        a_mla = _dot(omla_ref[0, rows, :], wmo_ref[...])
        g0 = _dot(u, wzg_ref[:, gate_cols(0)])
        yield
        a_gqa = _dot(ogqa_ref[0, rows, :], wgo_ref[...])
        g1 = _dot(u, wzg_ref[:, gate_cols(1)])
        y = _rms(y_ref[0, rows, :] * (z * jax.nn.sigmoid(z)), gssd_ref[...]).astype(BF16)
        yield
        g2 = _dot(u, wzg_ref[:, gate_cols(2)])
        a_ssd = _dot(y, wso_ref[...])
        merged = jax.nn.sigmoid(g0) * a_mla + jax.nn.sigmoid(g1) * a_gqa
        yield
        merged = (merged + jax.nn.sigmoid(g2) * a_ssd).astype(BF16)
        out = _dot(merged, wout_ref[...])
        yield
        o_ref[0, rows, :] = x + mod(5) * _rms(out, gpost_ref[...])

    _emit_skewed([strand(r0) for r0 in range(0, tm, OUT_SUB)])


def _mixer_out(xc, modx, modc, omla, ogqa, y, lw, *, n_lat, n_rows):
    b, t_all, d = xc.shape
    tm = OUT_SUB
    row = lambda w: pl.BlockSpec((1, tm, w), lambda i, j: (i, j, 0))
    res = [lw["wzg"], lw["g_ssd"], lw["w_mla_o"], lw["w_gqa_o"], lw["w_ssd_o"], lw["w_out"]]
    return pl.pallas_call(
        functools.partial(_out_kernel, tm=tm, n_lat=n_lat),
        grid=(b, n_rows // tm),
        in_specs=[row(d), pl.BlockSpec((1, N_MOD, d), lambda i, j: (i, 0, 0)), _resident((N_MOD, d)),
                  _resident((1, d)), _resident((1, d)),
                  row(MLA_HEADS * MLA_V), row(GQA_HEADS * GQA_HD), row(SSD_INNER)]
        + [_resident(w.shape) for w in res],
        out_specs=row(d),
        out_shape=jax.ShapeDtypeStruct((b, t_all, d), F32),
        input_output_aliases={0: 0},
        compiler_params=_cparams(2),
        name="mixer_out",
    )(xc, modx, modc, lw["g_pre1"], lw["g_post1"], omla, ogqa, y, *res)


def _rope_tables(n_lat, n_ctx):
    rows = n_lat // GRID_W
    r = jnp.repeat(jnp.arange(rows, dtype=F32), GRID_W)
    c = jnp.tile(jnp.arange(GRID_W, dtype=F32), rows)

    def angles(rot_dim):
        n_freq = rot_dim // 4
        inv = ROPE_THETA ** (-jnp.arange(n_freq, dtype=F32) / n_freq)
        return jnp.concatenate([r[:, None] * inv, c[:, None] * inv], axis=-1)

    def finish(cos_l, sin_l):
        ident_c = jnp.ones((n_ctx, LANE), F32)
        ident_s = jnp.zeros((n_ctx, LANE), F32)
        return jnp.concatenate([cos_l, ident_c], 0), jnp.concatenate([sin_l, ident_s], 0)

    ang = angles(MLA_ROPE)
    cos, sin = jnp.cos(ang), jnp.sin(ang)
    ones = jnp.ones((n_lat, MLA_NOPE), F32)
    zeros = jnp.zeros((n_lat, MLA_NOPE), F32)
    tail1 = jnp.ones((n_lat, HEAD_PAD - MLA_NOPE - MLA_ROPE), F32)
    tail0 = jnp.zeros((n_lat, HEAD_PAD - MLA_NOPE - MLA_ROPE), F32)
    mla = finish(jnp.concatenate([ones, cos, cos, tail1], 1), jnp.concatenate([zeros, -sin, sin, tail0], 1))

    ang = angles(GQA_HD)
    cos, sin = jnp.cos(ang), jnp.sin(ang)
    gqa = finish(jnp.concatenate([cos, cos, cos, cos], 1), jnp.concatenate([-sin, sin, -sin, sin], 1))
    return mla + gqa


def _head_indicator():
    head = jnp.arange(GQA_HEADS * GQA_HD) // GQA_HD
    e = (head[:, None] == jnp.arange(LANE)[None, :])
    return (e.astype(F32) / GQA_HD).astype(BF16)


def _group_lanes(v):
    return v.reshape(2, SSD_GROUPS, SSD_HPG).transpose(1, 0, 2).reshape(SSD_GROUPS, 2 * SSD_HPG)


def _layer_weights(l, p):
    d = p["w_in"].shape[1]
    w = p["w_in"][l]
    kg = w[:, _C_KG:_C_VG].reshape(d, GQA_KV_HEADS, GQA_HD)
    kdup = jnp.concatenate([kg, kg], axis=-1).reshape(d, 2 * GQA_KV_HEADS * GQA_HD)
    kr = jnp.zeros((d, HEAD_PAD), F32).at[:, MLA_NOPE:MLA_NOPE + MLA_ROPE].set(w[:, _C_KROPE:_C_KG])
    wdt = w[:, _C_DT:_C_QLAT].reshape(d, 2, SSD_GROUPS, SSD_HPG).transpose(0, 2, 1, 3).reshape(d, 2 * SSD_HEADS)
    wdt = jnp.pad(wdt, ((0, 0), (0, LANE - 2 * SSD_HEADS)))
    wa = jnp.concatenate([w[:, _C_KVLAT:_C_KROPE], kdup, w[:, _C_XBC:_C_DT], w[:, _C_QLAT:_C_QG],
                          w[:, _C_QG:_C_Z], kr, wdt], axis=1).astype(BF16)

    wkv = p["w_mla_kv_up"][l].reshape(MLA_KV_RANK, MLA_HEADS, MLA_NOPE + MLA_V)
    wk = jnp.pad(wkv[:, :, :MLA_NOPE], ((0, 0), (0, 0), (0, HEAD_PAD - MLA_NOPE)))
    wk = wk.reshape(MLA_KV_RANK, MLA_HEADS * HEAD_PAD).astype(BF16)
    wv_t = wkv[:, :, MLA_NOPE:].reshape(MLA_KV_RANK, MLA_HEADS * MLA_V).T.astype(BF16)
    wq = p["w_mla_q_up"][l].reshape(MLA_Q_RANK, MLA_HEADS, MLA_NOPE + MLA_ROPE)
    wq = jnp.pad(wq, ((0, 0), (0, 0), (0, HEAD_PAD - MLA_NOPE - MLA_ROPE)))
    wq = wq.reshape(MLA_Q_RANK, MLA_HEADS * HEAD_PAD).astype(BF16)

    par = jnp.zeros((SSD_GROUPS, 8, LANE), F32)
    par = par.at[:, 0, :2 * SSD_HPG].set(_group_lanes(p["dt_bias"][l]))
    par = par.at[:, 1, :2 * SSD_HPG].set(_group_lanes(p["a_log"][l]))
    skip = jnp.repeat(p["ssd_skip"][l].reshape(2, SSD_GROUPS, SSD_HPG), SSD_HD, axis=2).transpose(1, 0, 2)

    row = lambda v: v.reshape(1, -1).astype(F32)
    ffn = lambda s: (p["w_ffn_gate"][l, s].astype(BF16), p["w_ffn_up"][l, s].astype(BF16),
                     p["w_ffn_down"][l, s].astype(BF16))
    return {
        "ffn0": ffn(0), "ffn1": ffn(1),
        "g_pre0": row(p["g_pre"][l, 0]), "g_pre1": row(p["g_pre"][l, 1]), "g_pre2": row(p["g_pre"][l, 2]),
        "g_post0": row(p["g_post"][l, 0]), "g_post1": row(p["g_post"][l, 1]), "g_post2": row(p["g_post"][l, 2]),
        "wa": wa, "wvg_t": w[:, _C_VG:_C_XBC].T.astype(BF16),
        "g_mla_kv": row(p["g_mla_kv"][l]), "wk": wk, "wv_t": wv_t,
        "g_mla_q": row(p["g_mla_q"][l]), "wq": wq, "e": _head_indicator(),
        "g_gqa_k": row(jnp.tile(p["g_gqa_k"][l], 2 * GQA_KV_HEADS)),
        "g_gqa_q": row(jnp.tile(p["g_gqa_q"][l], GQA_HEADS)),
        "conv_w": p["conv_w"][l], "conv_b": row(p["conv_b"][l]), "ssd_par": par, "ssd_skip": skip,
        "wzg": w[:, _C_Z:].astype(BF16), "g_ssd": row(p["g_ssd"][l]),
        "w_mla_o": p["w_mla_o"][l].astype(BF16), "w_gqa_o": p["w_gqa_o"][l].astype(BF16),
        "w_ssd_o": p["w_ssd_o"][l].astype(BF16), "w_out": p["w_out"][l].astype(BF16),
    }


def kernel(x, c, ctx, c_ctx, w_mod, b_mod, g_pre, g_post, w_ffn_gate, w_ffn_up, w_ffn_down, w_in, g_mla_q, w_mla_q_up, g_mla_kv, w_mla_kv_up, g_gqa_q, g_gqa_k, conv_w, conv_b, dt_bias, a_log, ssd_skip, g_ssd, w_mla_o, w_gqa_o, w_ssd_o, w_out):
    b, n_lat, d = x.shape
    n_ctx = ctx.shape[1]
    t_all = n_lat + n_ctx
    depth = w_in.shape[0]
    assert n_lat % TOKEN_TILE == 0 and n_ctx % TOKEN_TILE == 0 and n_lat % GRID_W == 0
    assert w_in.shape[2] == _C_GATE + 3 * d and d == SSD_INNER
    p = dict(g_pre=g_pre, g_post=g_post, w_ffn_gate=w_ffn_gate, w_ffn_up=w_ffn_up, w_ffn_down=w_ffn_down,
             w_in=w_in, g_mla_q=g_mla_q, w_mla_q_up=w_mla_q_up, g_mla_kv=g_mla_kv, w_mla_kv_up=w_mla_kv_up,
             g_gqa_q=g_gqa_q, g_gqa_k=g_gqa_k, conv_w=conv_w, conv_b=conv_b, dt_bias=dt_bias, a_log=a_log,
             ssd_skip=ssd_skip, g_ssd=g_ssd, w_mla_o=w_mla_o, w_gqa_o=w_gqa_o, w_ssd_o=w_ssd_o, w_out=w_out)

    rows = -(-(b + 1) // 8) * 8
    c_all = jnp.concatenate([c, c_ctx[None, :], jnp.zeros((rows - b - 1, d), F32)], axis=0)
    mod = _modulation(c_all, w_mod, b_mod).reshape(depth, rows, N_MOD, d)
    tabs = _rope_tables(n_lat, n_ctx)
    xc = jnp.concatenate([x, ctx], axis=1)

    for l in range(depth):
        last = l == depth - 1
        lw = _layer_weights(l, p)
        modx, modc = mod[l, :b], mod[l, b]
        xc = _ffn(xc, modx, modc, lw["g_pre0"], lw["g_post0"], *lw["ffn0"],
                  k0=0, n_lat=n_lat, n_rows=t_all, alias=True)
        kmla, vtmla, qmla, kgqa, vtgqa, qgqa, xbc, dtr = _in_proj(xc, modx, modc, lw, tabs, n_lat=n_lat)
        y = _ssd(xbc, dtr, lw, n_lat=n_lat, n_ctx=n_ctx)
        omla = _attention("mla", qmla, kmla, vtmla, n_lat=n_lat, n_ctx=n_ctx, with_ctx_queries=not last)
        ogqa = _attention("gqa", qgqa, kgqa, vtgqa, n_lat=n_lat, n_ctx=n_ctx, with_ctx_queries=not last)
        n_rows = n_lat if last else t_all
        xc = _mixer_out(xc, modx, modc, omla, ogqa, y, lw, n_lat=n_lat, n_rows=n_rows)
        xc = _ffn(xc, modx, modc, lw["g_pre2"], lw["g_post2"], *lw["ffn1"],
                  k0=6, n_lat=n_lat, n_rows=n_rows, alias=not last)
    return xc
```

```python
import functools
import math

import jax
import jax.numpy as jnp
from jax import lax
from jax.experimental import pallas as pl
from jax.experimental.pallas import tpu as pltpu

F32 = jnp.float32
BF16 = jnp.bfloat16

GRID_W = 64
ROPE_THETA = 10000.0
NORM_EPS = 1e-6
FFN_RES = 0.5
N_MOD = 9

MLA_HEADS = 16
MLA_NOPE = 64
MLA_ROPE = 32
MLA_V = 64
MLA_Q_RANK = 512
MLA_KV_RANK = 256
GQA_HEADS = 16
GQA_KV_HEADS = 4
GQA_HD = 64
SSD_INNER = 1024
SSD_HD = 64
SSD_HEADS = 16
SSD_GROUPS = 4
SSD_HPG = 4
SSD_STATE = 128
SSD_CONV = 5
SSD_CHUNK = 128
SSD_CONV_DIM = SSD_INNER + 2 * SSD_GROUPS * SSD_STATE
MLA_SCALE = (MLA_NOPE + MLA_ROPE) ** -0.5
GQA_SCALE = GQA_HD ** -0.5
LOG2E = math.log2(math.e)

LANE = 128
HEAD_PAD = 128
TOKEN_TILE = 256
SUB_TILE = 256
OUT_SUB = 256
FFN_COLS = 768
KEY_CHUNK = 256
ATTN_HEADS = 16
SUM_ROWS = 16
ATTN_STAGES = (4, 8, 12)
VMEM_LIMIT = 56 * 1024 * 1024

_C_KVLAT = 0
_C_KROPE = _C_KVLAT + MLA_KV_RANK
_C_KG = _C_KROPE + MLA_ROPE
_C_VG = _C_KG + GQA_KV_HEADS * GQA_HD
_C_XBC = _C_VG + GQA_KV_HEADS * GQA_HD
_C_DT = _C_XBC + SSD_CONV_DIM
_C_QLAT = _C_DT + 2 * SSD_HEADS
_C_QG = _C_QLAT + MLA_Q_RANK
_C_Z = _C_QG + GQA_HEADS * GQA_HD
_C_GATE = _C_Z + SSD_INNER

_A_KVLAT = 0
_A_KDUP = _A_KVLAT + MLA_KV_RANK
_A_XBC = _A_KDUP + 2 * GQA_KV_HEADS * GQA_HD
_A_QLAT = _A_XBC + SSD_CONV_DIM
_A_QG = _A_QLAT + MLA_Q_RANK
_A_KR = _A_QG + GQA_HEADS * GQA_HD
_A_DT = _A_KR + HEAD_PAD
_A_END = _A_DT + LANE


def _cparams(n_axes):
    return pltpu.CompilerParams(
        dimension_semantics=("parallel",) * n_axes, vmem_limit_bytes=VMEM_LIMIT)


def _resident(shape):
    nd = len(shape)
    return pl.BlockSpec(shape, lambda *_: (0,) * nd, pipeline_mode=pl.Buffered(1))


def _rms(t, g):
    return t * lax.rsqrt(jnp.mean(t * t, axis=-1, keepdims=True) + NORM_EPS) * g


def _dot(a, b):
    return jnp.dot(a, b, preferred_element_type=F32)


def _dot_nt(a, b):
    return lax.dot_general(a, b, (((1,), (1,)), ((), ())), preferred_element_type=F32)


def _mod_rows(tile_idx, tm, n_lat, modx_ref, modc_ref, k, row0=0, n_rows=None):
    n_rows = tm if n_rows is None else n_rows
    rows = tile_idx * tm + row0 + lax.broadcasted_iota(jnp.int32, (n_rows, 1), 0)
    return jnp.where(rows >= n_lat, modc_ref[k:k + 1, :], modx_ref[0, k:k + 1, :])


def _emit_skewed(strands):
    pending, active = list(strands), []
    while pending or active:
        if pending:
            active.append(pending.pop(0))
        for s in list(active):
            if next(s, "done") == "done":
                active.remove(s)


def _mod_kernel(c_ref, w_ref, b_ref, o_ref):
    s = c_ref[...]
    s = s * jax.nn.sigmoid(s)
    o_ref[0] = jnp.dot(s, w_ref[0], preferred_element_type=F32,
                       precision=lax.Precision.HIGHEST) + b_ref[0]


def _modulation(c_all, w_mod, b_mod):
    n_layers, d, _ = w_mod.shape
    rows = c_all.shape[0]
    return pl.pallas_call(
        _mod_kernel,
        grid=(n_layers, N_MOD),
        in_specs=[
            pl.BlockSpec((rows, d), lambda l, j: (0, 0)),
            pl.BlockSpec((1, d, d), lambda l, j: (l, 0, j)),
            pl.BlockSpec((1, 1, d), lambda l, j: (l, 0, j)),
        ],
        out_specs=pl.BlockSpec((1, rows, d), lambda l, j: (l, 0, j)),
        out_shape=jax.ShapeDtypeStruct((n_layers, rows, N_MOD * d), F32),
        compiler_params=_cparams(2),
        name="modulation",
    )(c_all, w_mod, b_mod.reshape(n_layers, 1, N_MOD * d))


def _ffn_kernel(x_ref, modx_ref, modc_ref, gin_ref, gout_ref, wg_ref, wu_ref, wd_ref, o_ref,
                *, tm, n_lat, k0):
    t = pl.program_id(1)
    f = wg_ref.shape[1]
    cuts = list(range(0, f, FFN_COLS)) + [f]

    def strand(r0):
        rows = slice(r0, r0 + SUB_TILE)
        mod = lambda k: _mod_rows(t, tm, n_lat, modx_ref, modc_ref, k, r0, SUB_TILE)
        x = x_ref[0, rows, :]
        u = (_rms(x, gin_ref[...]) * (1.0 + mod(k0 + 1)) + mod(k0)).astype(BF16)
        yield
        hidden = []
        for lo, hi in zip(cuts[:-1], cuts[1:]):
            hidden.append((_dot(u, wg_ref[:, lo:hi]), _dot(u, wu_ref[:, lo:hi])))
            yield
        y = None
        for (hg, hu), lo, hi in zip(hidden, cuts[:-1], cuts[1:]):
            a = (hg * jax.nn.sigmoid(hg) * hu).astype(BF16)
            part = _dot(a, wd_ref[lo:hi, :])
            y = part if y is None else y + part
            yield
        o_ref[0, rows, :] = x + (FFN_RES * mod(k0 + 2)) * _rms(y, gout_ref[...])

    _emit_skewed([strand(r0) for r0 in range(0, tm, SUB_TILE)])


def _row_tile(n_rows):
    for tm in (3 * SUB_TILE, 2 * SUB_TILE):
        if n_rows % tm == 0:
            return tm
    return SUB_TILE


def _ffn(xc, modx, modc, g_in, g_out, wg, wu, wd, *, k0, n_lat, n_rows, alias):
    b, t_all, d = xc.shape
    f = wg.shape[1]
    tm = _row_tile(n_rows)
    out_rows = t_all if alias else n_rows
    return pl.pallas_call(
        functools.partial(_ffn_kernel, tm=tm, n_lat=n_lat, k0=k0),
        grid=(b, n_rows // tm),
        in_specs=[
            pl.BlockSpec((1, tm, d), lambda i, j: (i, j, 0)),
            pl.BlockSpec((1, N_MOD, d), lambda i, j: (i, 0, 0)),
            _resident((N_MOD, d)),
            _resident((1, d)),
            _resident((1, d)),
            _resident((d, f)),
            _resident((d, f)),
            _resident((f, d)),
        ],
        out_specs=pl.BlockSpec((1, tm, d), lambda i, j: (i, j, 0)),
        out_shape=jax.ShapeDtypeStruct((b, out_rows, d), F32),
        input_output_aliases={0: 0} if alias else {},
        compiler_params=_cparams(2),
        name="ffn_half_step",
    )(xc, modx, modc, g_in, g_out, wg, wu, wd)


def _tile_lanes(t, reps):
    return t if reps == 1 else jnp.concatenate([t] * reps, axis=1)


def _rope_gqa(x, cos_t, sin_t):
    w = x.shape[1]
    reps = w // LANE
    lane = lax.broadcasted_iota(jnp.int32, x.shape, 1) % GQA_HD
    half = GQA_HD // 2
    partner = jnp.where(lane < half, pltpu.roll(x, w - half, 1), pltpu.roll(x, half, 1))
    return x * _tile_lanes(cos_t, reps) + partner * _tile_lanes(sin_t, reps)


def _rope_mla(x, cos_t, sin_t):
    w = x.shape[1]
    reps = w // LANE
    lane = lax.broadcasted_iota(jnp.int32, x.shape, 1) % HEAD_PAD
    half = MLA_ROPE // 2
    first = jnp.logical_and(lane >= MLA_NOPE, lane < MLA_NOPE + half)
    partner = jnp.where(first, pltpu.roll(x, w - half, 1), pltpu.roll(x, half, 1))
    return x * _tile_lanes(cos_t, reps) + partner * _tile_lanes(sin_t, reps)


def _head_scale(x, ms, gain):
    r = lax.rsqrt(ms + NORM_EPS)
    lane = lax.broadcasted_iota(jnp.int32, r.shape, 1)
    second = (lane >= GQA_HD).astype(jnp.int32)
    tiles = [jnp.take_along_axis(r, 2 * j + second, axis=1) for j in range(x.shape[1] // LANE)]
    return x * jnp.concatenate(tiles, axis=1) * gain


def _in_proj_kernel(x_ref, modx_ref, modc_ref, gpre_ref, wa_ref, wvg_t_ref, gkv_ref, wk_ref, wv_t_ref,
                    gq_ref, wq_ref, e_ref, ggk_ref, ggq_ref,
                    mla_cos_ref, mla_sin_ref, gqa_cos_ref, gqa_sin_ref,
                    kmla_ref, vtmla_ref, qmla_ref, kgqa_ref, vtgqa_ref, qgqa_ref, xbc_ref, dt_ref,
                    *, tm, n_lat):
    t = pl.program_id(1)
    x = x_ref[0]
    shift = _mod_rows(t, tm, n_lat, modx_ref, modc_ref, 3)
    scale = _mod_rows(t, tm, n_lat, modx_ref, modc_ref, 4)
    u = (_rms(x, gpre_ref[...]) * (1.0 + scale) + shift).astype(BF16)
    proj = lambda lo, hi: _dot(u, wa_ref[:, lo:hi])
    mla_cos = mla_cos_ref[...]
    mla_sin = mla_sin_ref[...]
    gqa_cos = gqa_cos_ref[...]
    gqa_sin = gqa_sin_ref[...]
    n_k = 2 * GQA_KV_HEADS * GQA_HD

    h_kv = proj(_A_KVLAT, _A_KDUP)
    h_q = proj(_A_QLAT, _A_QG)
    h_kd = proj(_A_KDUP, _A_XBC)
    h_qg = proj(_A_QG, _A_KR)
    h_kr = proj(_A_KR, _A_END)
    ckv = _rms(h_kv, gkv_ref[...]).astype(BF16)
    cq = _rms(h_q, gq_ref[...]).astype(BF16)
    k_nope = _dot(ckv, wk_ref[...])
    v_t = _dot_nt(wv_t_ref[...], ckv)
    q_mla = _dot(cq, wq_ref[...])
    ms_k = _dot((h_kd * h_kd).astype(BF16), e_ref[0:n_k, :])
    ms_q = _dot((h_qg * h_qg).astype(BF16), e_ref[...])
    vg_t = _dot_nt(wvg_t_ref[...], u)

    dt_ref[0] = h_kr[:, HEAD_PAD:]
    k_rope = _rope_mla(h_kr[:, 0:HEAD_PAD], mla_cos, mla_sin)
    k_mla = (k_nope + _tile_lanes(k_rope, MLA_HEADS)).astype(BF16)
    for hd in range(MLA_HEADS):
        kmla_ref[0, hd] = k_mla[:, hd * HEAD_PAD:(hd + 1) * HEAD_PAD]
    vtmla_ref[0] = v_t.astype(BF16)
    qmla_ref[0] = (_rope_mla(q_mla, mla_cos, mla_sin) * (MLA_SCALE * LOG2E)).astype(BF16)
    kd = _head_scale(h_kd, ms_k, ggk_ref[...])
    kgqa_ref[0] = _rope_gqa(kd, gqa_cos, gqa_sin).astype(BF16)
    qg = _head_scale(h_qg, ms_q, ggq_ref[...])
    qgqa_ref[0] = (_rope_gqa(qg, gqa_cos, gqa_sin) * (GQA_SCALE * LOG2E)).astype(BF16)
    vtgqa_ref[0] = vg_t.astype(BF16)
    xbc_ref[0] = proj(_A_XBC, _A_QLAT)


def _in_proj(xc, modx, modc, lw, tabs, *, n_lat):
    b, t_all, d = xc.shape
    tm = TOKEN_TILE
    n_qm = MLA_HEADS * HEAD_PAD
    n_kd = 2 * GQA_KV_HEADS * GQA_HD
    n_qg = GQA_HEADS * GQA_HD
    n_vm = MLA_HEADS * MLA_V
    n_vg = GQA_KV_HEADS * GQA_HD
    row = lambda w: pl.BlockSpec((1, tm, w), lambda i, j: (i, j, 0))
    col = lambda h: pl.BlockSpec((1, h, tm), lambda i, j: (i, 0, j))
    tab = pl.BlockSpec((tm, LANE), lambda i, j: (j, 0))
    res = [lw["g_pre1"], lw["wa"], lw["wvg_t"], lw["g_mla_kv"], lw["wk"], lw["wv_t"],
           lw["g_mla_q"], lw["wq"], lw["e"], lw["g_gqa_k"], lw["g_gqa_q"]]
    return pl.pallas_call(
        functools.partial(_in_proj_kernel, tm=tm, n_lat=n_lat),
        grid=(b, t_all // tm),
        in_specs=[row(d), pl.BlockSpec((1, N_MOD, d), lambda i, j: (i, 0, 0)), _resident((N_MOD, d))]
        + [_resident(w.shape) for w in res] + [tab] * 4,
        out_specs=[pl.BlockSpec((1, MLA_HEADS, tm, HEAD_PAD), lambda i, j: (i, 0, j, 0)),
                   col(n_vm), row(n_qm), row(n_kd), col(n_vg), row(n_qg), row(SSD_CONV_DIM), row(LANE)],
        out_shape=[
            jax.ShapeDtypeStruct((b, MLA_HEADS, t_all, HEAD_PAD), BF16),
            jax.ShapeDtypeStruct((b, n_vm, t_all), BF16),
            jax.ShapeDtypeStruct((b, t_all, n_qm), BF16),
            jax.ShapeDtypeStruct((b, t_all, n_kd), BF16),
            jax.ShapeDtypeStruct((b, n_vg, t_all), BF16),
            jax.ShapeDtypeStruct((b, t_all, n_qg), BF16),
            jax.ShapeDtypeStruct((b, t_all, SSD_CONV_DIM), F32),
            jax.ShapeDtypeStruct((b, t_all, LANE), F32),
        ],
        compiler_params=_cparams(2),
        name="in_proj",
    )(xc, modx, modc, *res, *tabs)


_SSD_W = SSD_HPG * SSD_HD + 2 * SSD_STATE
_SSD_B0 = SSD_HPG * SSD_HD
_SSD_C0 = _SSD_B0 + SSD_STATE
_PAD_ROWS = 8
SSD_UNROLL = 3


def _softplus(v):
    return jnp.maximum(v, 0.0) + jnp.log1p(jnp.exp(-jnp.abs(v)))


def _ssd_kernel(xs_ref, bm_ref, cm_ref, dt_ref, wx_ref, wb_ref, wc_ref, bx_ref, bb_ref, bc_ref,
                par_ref, skip_ref, y_ref, pad_ref, act_ref, dtv_ref, e_ref, s_ref, h_ref, *, n_lat, n_ctx):
    cl = SSD_CHUNK
    t_all = n_lat + n_ctx
    n_chunks = t_all // cl
    n_ctx_chunks = n_ctx // cl
    n_lat_chunks = n_lat // cl
    g = pl.program_id(1)

    def pad_row(r):
        return pl.multiple_of(r + jnp.where(r >= n_lat, 2 * _PAD_ROWS, _PAD_ROWS), _PAD_ROWS)

    zeros = jnp.zeros((_PAD_ROWS, _SSD_W), F32)
    pad_ref[0:_PAD_ROWS, :] = zeros
    pad_ref[_PAD_ROWS + n_lat:2 * _PAD_ROWS + n_lat, :] = zeros
    pad_ref[2 * _PAD_ROWS + t_all:3 * _PAD_ROWS + t_all, :] = zeros

    def copy_body(i, carry):
        r = pl.multiple_of(i * cl, cl)
        dst = pad_row(r)
        pad_ref[pl.ds(dst, cl), 0:_SSD_B0] = xs_ref[0, pl.ds(r, cl), :]
        pad_ref[pl.ds(dst, cl), _SSD_B0:_SSD_C0] = bm_ref[0, pl.ds(r, cl), :]
        pad_ref[pl.ds(dst, cl), _SSD_C0:_SSD_W] = cm_ref[0, pl.ds(r, cl), :]
        return carry

    lax.fori_loop(0, n_chunks, copy_body, 0)

    conv_w = jnp.concatenate([wx_ref[...], wb_ref[...], wc_ref[...]], axis=1)
    conv_b = jnp.concatenate([bx_ref[...], bb_ref[...], bc_ref[...]], axis=1)
    par = par_ref[0]
    dt_bias = par[0:1, :]
    a_rate = -jnp.exp(par[1:2, :])
    lane_shift = lax.rem(LANE - 2 * SSD_HPG * g, LANE)
    half_win = SSD_CONV // 2

    def act_body(i, carry):
        r = pl.multiple_of(i * cl, cl)
        base = pad_row(r)
        blk = pad_ref[pl.ds(base - _PAD_ROWS, cl + 2 * _PAD_ROWS), :]
        acc = conv_b
        for k in range(SSD_CONV):
            off = _PAD_ROWS + k - half_win
            acc = acc + conv_w[k:k + 1, :] * blk[off:off + cl, :]
        act_ref[pl.ds(r, cl), :] = acc * jax.nn.sigmoid(acc)
        dtv = _softplus(pltpu.roll(dt_ref[0, pl.ds(r, cl), :], lane_shift, 1) + dt_bias)
        dtv_ref[pl.ds(r, cl), :] = dtv
        return carry

    lax.fori_loop(0, n_chunks, act_body, 0)

    gw = SSD_HPG * SSD_HD
    ri = lax.broadcasted_iota(jnp.int32, (cl, cl), 0)
    ci = lax.broadcasted_iota(jnp.int32, (cl, cl), 1)
    tris = [ri >= ci, ri <= ci]
    tri_bs = [jnp.where(t, 1.0, 0.0).astype(BF16) for t in tris]
    sl = lax.broadcasted_iota(jnp.int32, (LANE, gw), 0)
    sc = lax.broadcasted_iota(jnp.int32, (LANE, gw), 1) // SSD_HD
    sels = [jnp.where(sl == sc + d * SSD_HPG, 1.0, 0.0).astype(BF16) for d in range(2)]
    skip = skip_ref[0, 0:1, :] + skip_ref[0, 1:2, :]

    def split2(v):
        hi = v.astype(BF16)
        return hi, (v - hi.astype(F32)).astype(BF16)

    def local_strand(d, r, shared, out):
        dtc = dtv_ref[pl.ds(r, cl), :]
        a_hi, a_lo = split2(dtc * a_rate)
        cs = _dot(tri_bs[d], a_hi) + _dot(tri_bs[d], a_lo)
        yield
        cs_t = cs.T
        c_hi, c_lo = split2(cs)
        dt_hi, dt_lo = split2(dtc)
        cs_w = _dot(c_hi, sels[d]) + _dot(c_lo, sels[d])
        dt_w = _dot(dt_hi, sels[d]) + _dot(dt_lo, sels[d])
        yield
        cb, bt_b = shared
        tot_w = cs_w[0:1, :] if d else cs_w[cl - 1:cl, :]
        e_ref[d, pl.ds(r, cl), :] = jnp.exp(cs_w)
        xd = act_ref[pl.ds(r, cl), 0:_SSD_B0] * dt_w
        s_ref[d, pl.ds(r, cl), :] = _dot(bt_b, (xd * jnp.exp(tot_w - cs_w)).astype(BF16))
        xd_b = xd.astype(BF16)
        yield
        first_half = lax.broadcasted_iota(jnp.int32, (cl, LANE), 1) < SSD_HD
        tiles = []
        for pair in range(SSD_HPG // 2):
            prods = []
            for j in (2 * pair, 2 * pair + 1):
                ln = d * SSD_HPG + j
                lmat = jnp.exp(jnp.where(tris[d], cs[:, ln:ln + 1] - cs_t[ln:ln + 1, :], -jnp.inf))
                prods.append(_dot((cb * lmat).astype(BF16), xd_b[:, pair * LANE:(pair + 1) * LANE]))
                yield
            tiles.append(jnp.where(first_half, prods[0], prods[1]))
        out.append(jnp.concatenate(tiles, axis=1))

    def local_body(it, carry):
        rows = [pl.multiple_of((it * SSD_UNROLL + u) * cl, cl) for u in range(SSD_UNROLL)]
        outs = [[] for _ in rows]
        shared = []
        for r in rows:
            bmat = act_ref[pl.ds(r, cl), _SSD_B0:_SSD_C0]
            c_b = act_ref[pl.ds(r, cl), _SSD_C0:_SSD_W].astype(BF16)
            shared.append((_dot_nt(c_b, bmat.astype(BF16)), bmat.T.astype(BF16)))
        strands = [local_strand(d, r, sh, o) for r, sh, o in zip(rows, shared, outs) for d in range(2)]
        while strands:
            for s in list(strands):
                if next(s, "done") == "done":
                    strands.remove(s)
        for r, o in zip(rows, outs):
            y_ref[0, pl.ds(r, cl), :] = skip * act_ref[pl.ds(r, cl), 0:_SSD_B0] + o[0] + o[1]
        return carry

    lax.fori_loop(0, n_chunks // SSD_UNROLL, local_body, 0)

    h_ref[...] = jnp.zeros(h_ref.shape, F32)

    def scan_body(it, carry):
        hs = [h_ref[0], h_ref[1]]
        todo = []
        for u in range(SSD_UNROLL):
            i = it * SSD_UNROLL + u
            c_fwd = jnp.where(i < n_ctx_chunks, n_lat_chunks + i, i - n_ctx_chunks)
            for d, c in ((0, c_fwd), (1, n_chunks - 1 - i)):
                r = pl.multiple_of(c * cl, cl)
                c_b = act_ref[pl.ds(r, cl), _SSD_C0:_SSD_W].astype(BF16)
                y_off = _dot(c_b, hs[d].astype(BF16)) * e_ref[d, pl.ds(r, cl), :]
                decay = e_ref[d, pl.ds(r + (0 if d else cl - 1), 1), :]
                hs[d] = decay * hs[d] + s_ref[d, pl.ds(r, cl), :]
                todo.append((r, y_off))
        h_ref[0] = hs[0]
        h_ref[1] = hs[1]
        for r, y_off in todo:
            y_ref[0, pl.ds(r, cl), :] = y_ref[0, pl.ds(r, cl), :] + y_off
        return carry

    lax.fori_loop(0, n_chunks // SSD_UNROLL, scan_body, 0)


def _ssd(xbc, dtr, lw, *, n_lat, n_ctx):
    b, t_all, _ = xbc.shape
    gw = SSD_HPG * SSD_HD
    assert (t_all // SSD_CHUNK) % SSD_UNROLL == 0
    n_x = SSD_INNER // SSD_STATE
    xs_blk = lambda i, g: (i, 0, g)
    b_blk = lambda i, g: (i, 0, n_x + g)
    c_blk = lambda i, g: (i, 0, n_x + SSD_GROUPS + g)
    w_cols = lambda rows: [pl.BlockSpec((rows, gw), lambda i, g: (0, g)),
                           pl.BlockSpec((rows, SSD_STATE), lambda i, g: (0, n_x + g)),
                           pl.BlockSpec((rows, SSD_STATE), lambda i, g: (0, n_x + SSD_GROUPS + g))]
    return pl.pallas_call(
        functools.partial(_ssd_kernel, n_lat=n_lat, n_ctx=n_ctx),
        grid=(b, SSD_GROUPS),
        in_specs=[
            pl.BlockSpec((1, t_all, gw), xs_blk),
            pl.BlockSpec((1, t_all, SSD_STATE), b_blk),
            pl.BlockSpec((1, t_all, SSD_STATE), c_blk),
            pl.BlockSpec((1, t_all, LANE), lambda i, g: (i, 0, 0)),
        ] + w_cols(SSD_CONV) + w_cols(1) + [
            pl.BlockSpec((1, 8, LANE), lambda i, g: (g, 0, 0)),
            pl.BlockSpec((1, 2, gw), lambda i, g: (g, 0, 0)),
        ],
        out_specs=pl.BlockSpec((1, t_all, gw), xs_blk),
        out_shape=jax.ShapeDtypeStruct((b, t_all, SSD_INNER), F32),
        scratch_shapes=[
            pltpu.VMEM((t_all + 3 * _PAD_ROWS, _SSD_W), F32),
            pltpu.VMEM((t_all, _SSD_W), F32),
            pltpu.VMEM((t_all, LANE), F32),
            pltpu.VMEM((2, t_all, gw), F32),
            pltpu.VMEM((2, t_all, gw), F32),
            pltpu.VMEM((2, SSD_STATE, gw), F32),
        ],
        compiler_params=_cparams(2),
        name="ssd_scan",
    )(xbc, xbc, xbc, dtr, lw["conv_w"], lw["conv_w"], lw["conv_w"], lw["conv_b"], lw["conv_b"],
      lw["conv_b"], lw["ssd_par"], lw["ssd_skip"])


def _attend_heads(qs, load_ks, load_vts, k_lo, k_hi):
    n = len(qs)
    tasks = [(h, lo, min(lo + KEY_CHUNK, k_hi)) for lo in range(k_lo, k_hi, KEY_CHUNK) for h in range(n)]
    nt = len(tasks)
    m, l, acc = [None] * n, [None] * n, [None] * n
    s, shift, alpha, p = {}, {}, {}, {}
    d_max, d_exp, d_pv = ATTN_STAGES
    for i in range(nt + d_pv):
        if i < nt:
            h, lo, hi = tasks[i]
            s[i] = _dot_nt(load_ks[h](lo, hi), qs[h])
        j = i - d_max
        if 0 <= j < nt:
            h = tasks[j][0]
            cm = jnp.max(s[j], axis=0, keepdims=True)
            if m[h] is None:
                alpha[j] = None
                m[h] = cm
            else:
                m_new = jnp.maximum(m[h], cm)
                alpha[j] = jnp.exp2(m[h] - m_new)
                m[h] = m_new
            shift[j] = m[h]
        j = i - d_exp
        if 0 <= j < nt:
            p[j] = jnp.exp2(s.pop(j) - shift.pop(j)).astype(BF16)
        j = i - d_pv
        if 0 <= j < nt:
            h, lo, hi = tasks[j]
            v_ext = jnp.concatenate([load_vts[h](lo, hi), jnp.ones((SUM_ROWS, hi - lo), BF16)], axis=0)
            pv = _dot(v_ext, p.pop(j))
            a = alpha.pop(j)
            acc[h] = pv if a is None else a * acc[h] + pv
    dv = acc[0].shape[0] - SUM_ROWS
    return [acc[h][0:dv, :] / acc[h][dv:dv + 1, :] for h in range(n)]


def _by_query_tile(compute, n_lat, n_ctx, nq_ctx, with_ctx_queries):
    if not with_ctx_queries:
        compute(0, n_lat + n_ctx)
        return
    qi = pl.program_id(2)

    @pl.when(qi >= nq_ctx)
    def _():
        compute(0, n_lat + n_ctx)

    @pl.when(qi < nq_ctx)
    def _():
        compute(n_lat, n_lat + n_ctx)


def _store_heads(o_ref, outs):
    for i in range(0, len(outs), 2):
        pair = jnp.concatenate(outs[i:i + 2], axis=0).T.astype(BF16)
        o_ref[0, :, i // 2 * LANE:(i // 2 + 1) * LANE] = pair


def _mla_attn_kernel(q_ref, k_ref, vt_ref, o_ref, *, n_lat, n_ctx, nq_ctx, with_ctx_queries):
    def compute(k_lo, k_hi):
        lanes = [slice(i * HEAD_PAD, (i + 1) * HEAD_PAD) for i in range(ATTN_HEADS)]
        rows = [slice(i * MLA_V, (i + 1) * MLA_V) for i in range(ATTN_HEADS)]
        outs = _attend_heads([q_ref[0, :, ln] for ln in lanes],
                             [lambda lo, hi, i=i: k_ref[0, i, lo:hi, :] for i in range(ATTN_HEADS)],
                             [lambda lo, hi, rw=rw: vt_ref[0, rw, lo:hi] for rw in rows], k_lo, k_hi)
        _store_heads(o_ref, outs)

    _by_query_tile(compute, n_lat, n_ctx, nq_ctx, with_ctx_queries)


def _gqa_attn_kernel(q_ref, k_ref, vt_ref, o_ref, *, n_lat, n_ctx, nq_ctx, with_ctx_queries):
    def compute(k_lo, k_hi):
        qs = []
        for i in range(ATTN_HEADS):
            q2 = q_ref[0, :, i // 2 * LANE:(i // 2 + 1) * LANE]
            lane = lax.broadcasted_iota(jnp.int32, q2.shape, 1)
            sel = (lane < GQA_HD) if i % 2 == 0 else (lane >= GQA_HD)
            qs.append(jnp.where(sel, q2, jnp.zeros_like(q2)))
        rep = GQA_HEADS // GQA_KV_HEADS
        k_lanes = [slice(i // rep * LANE, (i // rep + 1) * LANE) for i in range(ATTN_HEADS)]
        v_rows = [slice(i // rep * GQA_HD, (i // rep + 1) * GQA_HD) for i in range(ATTN_HEADS)]
        outs = _attend_heads(qs, [lambda lo, hi, ln=ln: k_ref[0, lo:hi, ln] for ln in k_lanes],
                             [lambda lo, hi, rw=rw: vt_ref[0, rw, lo:hi] for rw in v_rows], k_lo, k_hi)
        _store_heads(o_ref, outs)

    _by_query_tile(compute, n_lat, n_ctx, nq_ctx, with_ctx_queries)


def _attention(kind, q, k, v_t, *, n_lat, n_ctx, with_ctx_queries):
    b, t_all, _ = q.shape
    tq = TOKEN_TILE
    nq_lat = n_lat // tq
    nq = t_all // tq if with_ctx_queries else nq_lat
    q_tile = lambda j: (j + nq_lat) % nq
    nh = ATTN_HEADS
    rep = GQA_HEADS // GQA_KV_HEADS
    assert nh % rep == 0 and MLA_HEADS % nh == 0
    if kind == "mla":
        body = _mla_attn_kernel
        in_specs = [
            pl.BlockSpec((1, tq, nh * HEAD_PAD), lambda i, p, j: (i, q_tile(j), p)),
            pl.BlockSpec((1, nh, t_all, HEAD_PAD), lambda i, p, j: (i, p, 0, 0)),
            pl.BlockSpec((1, nh * MLA_V, t_all), lambda i, p, j: (i, p, 0)),
        ]
    else:
        body = _gqa_attn_kernel
        in_specs = [
            pl.BlockSpec((1, tq, nh * GQA_HD), lambda i, p, j: (i, q_tile(j), p)),
            pl.BlockSpec((1, t_all, nh // rep * 2 * GQA_HD), lambda i, p, j: (i, 0, p)),
            pl.BlockSpec((1, nh // rep * GQA_HD, t_all), lambda i, p, j: (i, p, 0)),
        ]
    return pl.pallas_call(
        functools.partial(body, n_lat=n_lat, n_ctx=n_ctx, nq_ctx=nq - nq_lat, with_ctx_queries=with_ctx_queries),
        grid=(b, MLA_HEADS // nh, nq),
        in_specs=in_specs,
        out_specs=pl.BlockSpec((1, tq, nh * MLA_V), lambda i, p, j: (i, q_tile(j), p)),
        out_shape=jax.ShapeDtypeStruct((b, t_all, MLA_HEADS * MLA_V), BF16),
        compiler_params=_cparams(3),
        name=kind + "_attention",
    )(q, k, v_t)


def _out_kernel(x_ref, modx_ref, modc_ref, gpre_ref, gpost_ref, omla_ref, ogqa_ref, y_ref, wzg_ref, gssd_ref,
                wmo_ref, wgo_ref, wso_ref, wout_ref, o_ref, *, tm, n_lat):
    t = pl.program_id(1)
    d = x_ref.shape[2]
    gate_cols = lambda i: slice(SSD_INNER + i * d, SSD_INNER + (i + 1) * d)

    def strand(r0):
        rows = slice(r0, r0 + OUT_SUB)
        mod = lambda k: _mod_rows(t, tm, n_lat, modx_ref, modc_ref, k, r0, OUT_SUB)
        x = x_ref[0, rows, :]
        u = (_rms(x, gpre_ref[...]) * (1.0 + mod(4)) + mod(3)).astype(BF16)
        yield
        z = _dot(u, wzg_ref[:, 0:SSD_INNER])
        a_mla = _dot(omla_ref[0, rows, :], wmo_ref[...])
        g0 = _dot(u, wzg_ref[:, gate_cols(0)])
        yield
        a_gqa = _dot(ogqa_ref[0, rows, :], wgo_ref[...])
        g1 = _dot(u, wzg_ref[:, gate_cols(1)])
        y = _rms(y_ref[0, rows, :] * (z * jax.nn.sigmoid(z)), gssd_ref[...]).astype(BF16)
        yield
        g2 = _dot(u, wzg_ref[:, gate_cols(2)])
        a_ssd = _dot(y, wso_ref[...])
        merged = jax.nn.sigmoid(g0) * a_mla + jax.nn.sigmoid(g1) * a_gqa
        yield
        merged = (merged + jax.nn.sigmoid(g2) * a_ssd).astype(BF16)
        out = _dot(merged, wout_ref[...])
        yield
        o_ref[0, rows, :] = x + mod(5) * _rms(out, gpost_ref[...])

    _emit_skewed([strand(r0) for r0 in range(0, tm, OUT_SUB)])


def _mixer_out(xc, modx, modc, omla, ogqa, y, lw, *, n_lat, n_rows):
    b, t_all, d = xc.shape
    tm = OUT_SUB
    row = lambda w: pl.BlockSpec((1, tm, w), lambda i, j: (i, j, 0))
    res = [lw["wzg"], lw["g_ssd"], lw["w_mla_o"], lw["w_gqa_o"], lw["w_ssd_o"], lw["w_out"]]
    return pl.pallas_call(
        functools.partial(_out_kernel, tm=tm, n_lat=n_lat),
        grid=(b, n_rows // tm),
        in_specs=[row(d), pl.BlockSpec((1, N_MOD, d), lambda i, j: (i, 0, 0)), _resident((N_MOD, d)),
                  _resident((1, d)), _resident((1, d)),
                  row(MLA_HEADS * MLA_V), row(GQA_HEADS * GQA_HD), row(SSD_INNER)]
        + [_resident(w.shape) for w in res],
        out_specs=row(d),
        out_shape=jax.ShapeDtypeStruct((b, t_all, d), F32),
        input_output_aliases={0: 0},
        compiler_params=_cparams(2),
        name="mixer_out",
    )(xc, modx, modc, lw["g_pre1"], lw["g_post1"], omla, ogqa, y, *res)


def _rope_tables(n_lat, n_ctx):
    rows = n_lat // GRID_W
    r = jnp.repeat(jnp.arange(rows, dtype=F32), GRID_W)
    c = jnp.tile(jnp.arange(GRID_W, dtype=F32), rows)

    def angles(rot_dim):
        n_freq = rot_dim // 4
        inv = ROPE_THETA ** (-jnp.arange(n_freq, dtype=F32) / n_freq)
        return jnp.concatenate([r[:, None] * inv, c[:, None] * inv], axis=-1)

    def finish(cos_l, sin_l):
        ident_c = jnp.ones((n_ctx, LANE), F32)
        ident_s = jnp.zeros((n_ctx, LANE), F32)
        return jnp.concatenate([cos_l, ident_c], 0), jnp.concatenate([sin_l, ident_s], 0)

    ang = angles(MLA_ROPE)
    cos, sin = jnp.cos(ang), jnp.sin(ang)
    ones = jnp.ones((n_lat, MLA_NOPE), F32)
    zeros = jnp.zeros((n_lat, MLA_NOPE), F32)
    tail1 = jnp.ones((n_lat, HEAD_PAD - MLA_NOPE - MLA_ROPE), F32)
    tail0 = jnp.zeros((n_lat, HEAD_PAD - MLA_NOPE - MLA_ROPE), F32)
    mla = finish(jnp.concatenate([ones, cos, cos, tail1], 1), jnp.concatenate([zeros, -sin, sin, tail0], 1))

    ang = angles(GQA_HD)
    cos, sin = jnp.cos(ang), jnp.sin(ang)
    gqa = finish(jnp.concatenate([cos, cos, cos, cos], 1), jnp.concatenate([-sin, sin, -sin, sin], 1))
    return mla + gqa


def _head_indicator():
    head = jnp.arange(GQA_HEADS * GQA_HD) // GQA_HD
    e = (head[:, None] == jnp.arange(LANE)[None, :])
    return (e.astype(F32) / GQA_HD).astype(BF16)


def _group_lanes(v):
    return v.reshape(2, SSD_GROUPS, SSD_HPG).transpose(1, 0, 2).reshape(SSD_GROUPS, 2 * SSD_HPG)


def _layer_weights(l, p):
    d = p["w_in"].shape[1]
    w = p["w_in"][l]
    kg = w[:, _C_KG:_C_VG].reshape(d, GQA_KV_HEADS, GQA_HD)
    kdup = jnp.concatenate([kg, kg], axis=-1).reshape(d, 2 * GQA_KV_HEADS * GQA_HD)
    kr = jnp.zeros((d, HEAD_PAD), F32).at[:, MLA_NOPE:MLA_NOPE + MLA_ROPE].set(w[:, _C_KROPE:_C_KG])
    wdt = w[:, _C_DT:_C_QLAT].reshape(d, 2, SSD_GROUPS, SSD_HPG).transpose(0, 2, 1, 3).reshape(d, 2 * SSD_HEADS)
    wdt = jnp.pad(wdt, ((0, 0), (0, LANE - 2 * SSD_HEADS)))
    wa = jnp.concatenate([w[:, _C_KVLAT:_C_KROPE], kdup, w[:, _C_XBC:_C_DT], w[:, _C_QLAT:_C_QG],
                          w[:, _C_QG:_C_Z], kr, wdt], axis=1).astype(BF16)

    wkv = p["w_mla_kv_up"][l].reshape(MLA_KV_RANK, MLA_HEADS, MLA_NOPE + MLA_V)
    wk = jnp.pad(wkv[:, :, :MLA_NOPE], ((0, 0), (0, 0), (0, HEAD_PAD - MLA_NOPE)))
    wk = wk.reshape(MLA_KV_RANK, MLA_HEADS * HEAD_PAD).astype(BF16)
    wv_t = wkv[:, :, MLA_NOPE:].reshape(MLA_KV_RANK, MLA_HEADS * MLA_V).T.astype(BF16)
    wq = p["w_mla_q_up"][l].reshape(MLA_Q_RANK, MLA_HEADS, MLA_NOPE + MLA_ROPE)
    wq = jnp.pad(wq, ((0, 0), (0, 0), (0, HEAD_PAD - MLA_NOPE - MLA_ROPE)))
    wq = wq.reshape(MLA_Q_RANK, MLA_HEADS * HEAD_PAD).astype(BF16)

    par = jnp.zeros((SSD_GROUPS, 8, LANE), F32)
    par = par.at[:, 0, :2 * SSD_HPG].set(_group_lanes(p["dt_bias"][l]))
    par = par.at[:, 1, :2 * SSD_HPG].set(_group_lanes(p["a_log"][l]))
    skip = jnp.repeat(p["ssd_skip"][l].reshape(2, SSD_GROUPS, SSD_HPG), SSD_HD, axis=2).transpose(1, 0, 2)

    row = lambda v: v.reshape(1, -1).astype(F32)
    ffn = lambda s: (p["w_ffn_gate"][l, s].astype(BF16), p["w_ffn_up"][l, s].astype(BF16),
                     p["w_ffn_down"][l, s].astype(BF16))
    return {
        "ffn0": ffn(0), "ffn1": ffn(1),
        "g_pre0": row(p["g_pre"][l, 0]), "g_pre1": row(p["g_pre"][l, 1]), "g_pre2": row(p["g_pre"][l, 2]),
        "g_post0": row(p["g_post"][l, 0]), "g_post1": row(p["g_post"][l, 1]), "g_post2": row(p["g_post"][l, 2]),
        "wa": wa, "wvg_t": w[:, _C_VG:_C_XBC].T.astype(BF16),
        "g_mla_kv": row(p["g_mla_kv"][l]), "wk": wk, "wv_t": wv_t,
        "g_mla_q": row(p["g_mla_q"][l]), "wq": wq, "e": _head_indicator(),
        "g_gqa_k": row(jnp.tile(p["g_gqa_k"][l], 2 * GQA_KV_HEADS)),
        "g_gqa_q": row(jnp.tile(p["g_gqa_q"][l], GQA_HEADS)),
        "conv_w": p["conv_w"][l], "conv_b": row(p["conv_b"][l]), "ssd_par": par, "ssd_skip": skip,
        "wzg": w[:, _C_Z:].astype(BF16), "g_ssd": row(p["g_ssd"][l]),
        "w_mla_o": p["w_mla_o"][l].astype(BF16), "w_gqa_o": p["w_gqa_o"][l].astype(BF16),
        "w_ssd_o": p["w_ssd_o"][l].astype(BF16), "w_out": p["w_out"][l].astype(BF16),
    }


def kernel(x, c, ctx, c_ctx, w_mod, b_mod, g_pre, g_post, w_ffn_gate, w_ffn_up, w_ffn_down, w_in, g_mla_q, w_mla_q_up, g_mla_kv, w_mla_kv_up, g_gqa_q, g_gqa_k, conv_w, conv_b, dt_bias, a_log, ssd_skip, g_ssd, w_mla_o, w_gqa_o, w_ssd_o, w_out):
    b, n_lat, d = x.shape
    n_ctx = ctx.shape[1]
    t_all = n_lat + n_ctx
    depth = w_in.shape[0]
    assert n_lat % TOKEN_TILE == 0 and n_ctx % TOKEN_TILE == 0 and n_lat % GRID_W == 0
    assert w_in.shape[2] == _C_GATE + 3 * d and d == SSD_INNER
    p = dict(g_pre=g_pre, g_post=g_post, w_ffn_gate=w_ffn_gate, w_ffn_up=w_ffn_up, w_ffn_down=w_ffn_down,
             w_in=w_in, g_mla_q=g_mla_q, w_mla_q_up=w_mla_q_up, g_mla_kv=g_mla_kv, w_mla_kv_up=w_mla_kv_up,
             g_gqa_q=g_gqa_q, g_gqa_k=g_gqa_k, conv_w=conv_w, conv_b=conv_b, dt_bias=dt_bias, a_log=a_log,
             ssd_skip=ssd_skip, g_ssd=g_ssd, w_mla_o=w_mla_o, w_gqa_o=w_gqa_o, w_ssd_o=w_ssd_o, w_out=w_out)

    rows = -(-(b + 1) // 8) * 8
    c_all = jnp.concatenate([c, c_ctx[None, :], jnp.zeros((rows - b - 1, d), F32)], axis=0)
    mod = _modulation(c_all, w_mod, b_mod).reshape(depth, rows, N_MOD, d)
    tabs = _rope_tables(n_lat, n_ctx)
    xc = jnp.concatenate([x, ctx], axis=1)

    for l in range(depth):
        last = l == depth - 1
        lw = _layer_weights(l, p)
        modx, modc = mod[l, :b], mod[l, b]
        xc = _ffn(xc, modx, modc, lw["g_pre0"], lw["g_post0"], *lw["ffn0"],
                  k0=0, n_lat=n_lat, n_rows=t_all, alias=True)
        kmla, vtmla, qmla, kgqa, vtgqa, qgqa, xbc, dtr = _in_proj(xc, modx, modc, lw, tabs, n_lat=n_lat)
        y = _ssd(xbc, dtr, lw, n_lat=n_lat, n_ctx=n_ctx)
        omla = _attention("mla", qmla, kmla, vtmla, n_lat=n_lat, n_ctx=n_ctx, with_ctx_queries=not last)
        ogqa = _attention("gqa", qgqa, kgqa, vtgqa, n_lat=n_lat, n_ctx=n_ctx, with_ctx_queries=not last)
        n_rows = n_lat if last else t_all
        xc = _mixer_out(xc, modx, modc, omla, ogqa, y, lw, n_lat=n_lat, n_rows=n_rows)
        xc = _ffn(xc, modx, modc, lw["g_pre2"], lw["g_post2"], *lw["ffn1"],
                  k0=6, n_lat=n_lat, n_rows=n_rows, alias=not last)
    return xc
```

```python
import functools
import math

import jax
import jax.numpy as jnp
from jax import lax
from jax.experimental import pallas as pl
from jax.experimental.pallas import tpu as pltpu

F32 = jnp.float32
BF16 = jnp.bfloat16

GRID_W = 64
ROPE_THETA = 10000.0
NORM_EPS = 1e-6
FFN_RES = 0.5
N_MOD = 9

MLA_HEADS = 16
MLA_NOPE = 64
MLA_ROPE = 32
MLA_V = 64
MLA_Q_RANK = 512
MLA_KV_RANK = 256
GQA_HEADS = 16
GQA_KV_HEADS = 4
GQA_HD = 64
SSD_INNER = 1024
SSD_HD = 64
SSD_HEADS = 16
SSD_GROUPS = 4
SSD_HPG = 4
SSD_STATE = 128
SSD_CONV = 5
SSD_CHUNK = 128
SSD_CONV_DIM = SSD_INNER + 2 * SSD_GROUPS * SSD_STATE
MLA_SCALE = (MLA_NOPE + MLA_ROPE) ** -0.5
GQA_SCALE = GQA_HD ** -0.5
LOG2E = math.log2(math.e)

LANE = 128
HEAD_PAD = 128
TOKEN_TILE = 256
SUB_TILE = 256
OUT_SUB = 256
FFN_COLS = 768
KEY_CHUNK = 256
ATTN_HEADS = 16
SUM_ROWS = 16
ATTN_STAGES = (4, 8, 12)
VMEM_LIMIT = 56 * 1024 * 1024

_C_KVLAT = 0
_C_KROPE = _C_KVLAT + MLA_KV_RANK
_C_KG = _C_KROPE + MLA_ROPE
_C_VG = _C_KG + GQA_KV_HEADS * GQA_HD
_C_XBC = _C_VG + GQA_KV_HEADS * GQA_HD
_C_DT = _C_XBC + SSD_CONV_DIM
_C_QLAT = _C_DT + 2 * SSD_HEADS
_C_QG = _C_QLAT + MLA_Q_RANK
_C_Z = _C_QG + GQA_HEADS * GQA_HD
_C_GATE = _C_Z + SSD_INNER

_A_KVLAT = 0
_A_KDUP = _A_KVLAT + MLA_KV_RANK
_A_XBC = _A_KDUP + 2 * GQA_KV_HEADS * GQA_HD
_A_QLAT = _A_XBC + SSD_CONV_DIM
_A_QG = _A_QLAT + MLA_Q_RANK
_A_KR = _A_QG + GQA_HEADS * GQA_HD
_A_DT = _A_KR + HEAD_PAD
_A_END = _A_DT + LANE


def _cparams(n_axes):
    return pltpu.CompilerParams(
        dimension_semantics=("parallel",) * n_axes, vmem_limit_bytes=VMEM_LIMIT)


def _resident(shape):
    nd = len(shape)
    return pl.BlockSpec(shape, lambda *_: (0,) * nd, pipeline_mode=pl.Buffered(1))


def _rms(t, g):
    return t * lax.rsqrt(jnp.mean(t * t, axis=-1, keepdims=True) + NORM_EPS) * g


def _dot(a, b):
    return jnp.dot(a, b, preferred_element_type=F32)


def _dot_nt(a, b):
    return lax.dot_general(a, b, (((1,), (1,)), ((), ())), preferred_element_type=F32)


def _mod_rows(tile_idx, tm, n_lat, modx_ref, modc_ref, k, row0=0, n_rows=None):
    n_rows = tm if n_rows is None else n_rows
    rows = tile_idx * tm + row0 + lax.broadcasted_iota(jnp.int32, (n_rows, 1), 0)
    return jnp.where(rows >= n_lat, modc_ref[k:k + 1, :], modx_ref[0, k:k + 1, :])


def _emit_skewed(strands):
    pending, active = list(strands), []
    while pending or active:
        if pending:
            active.append(pending.pop(0))
        for s in list(active):
            if next(s, "done") == "done":
                active.remove(s)


def _mod_kernel(c_ref, w_ref, b_ref, o_ref):
    s = c_ref[...]
    s = s * jax.nn.sigmoid(s)
    o_ref[0] = jnp.dot(s, w_ref[0], preferred_element_type=F32,
                       precision=lax.Precision.HIGHEST) + b_ref[0]


def _modulation(c_all, w_mod, b_mod):
    n_layers, d, _ = w_mod.shape
    rows = c_all.shape[0]
    return pl.pallas_call(
        _mod_kernel,
        grid=(n_layers, N_MOD),
        in_specs=[
            pl.BlockSpec((rows, d), lambda l, j: (0, 0)),
            pl.BlockSpec((1, d, d), lambda l, j: (l, 0, j)),
            pl.BlockSpec((1, 1, d), lambda l, j: (l, 0, j)),
        ],
        out_specs=pl.BlockSpec((1, rows, d), lambda l, j: (l, 0, j)),
        out_shape=jax.ShapeDtypeStruct((n_layers, rows, N_MOD * d), F32),
        compiler_params=_cparams(2),
        name="modulation",
    )(c_all, w_mod, b_mod.reshape(n_layers, 1, N_MOD * d))


def _ffn_kernel(x_ref, modx_ref, modc_ref, gin_ref, gout_ref, wg_ref, wu_ref, wd_ref, o_ref,
                *, tm, n_lat, k0):
    t = pl.program_id(1)
    f = wg_ref.shape[1]
    cuts = list(range(0, f, FFN_COLS)) + [f]

    def strand(r0):
        rows = slice(r0, r0 + SUB_TILE)
        mod = lambda k: _mod_rows(t, tm, n_lat, modx_ref, modc_ref, k, r0, SUB_TILE)
        x = x_ref[0, rows, :]
        u = (_rms(x, gin_ref[...]) * (1.0 + mod(k0 + 1)) + mod(k0)).astype(BF16)
        yield
        hidden = []
        for lo, hi in zip(cuts[:-1], cuts[1:]):
            hidden.append((_dot(u, wg_ref[:, lo:hi]), _dot(u, wu_ref[:, lo:hi])))
            yield
        y = None
        for (hg, hu), lo, hi in zip(hidden, cuts[:-1], cuts[1:]):
            a = (hg * jax.nn.sigmoid(hg) * hu).astype(BF16)
            part = _dot(a, wd_ref[lo:hi, :])
            y = part if y is None else y + part
            yield
        o_ref[0, rows, :] = x + (FFN_RES * mod(k0 + 2)) * _rms(y, gout_ref[...])

    _emit_skewed([strand(r0) for r0 in range(0, tm, SUB_TILE)])


def _row_tile(n_rows):
    for tm in (3 * SUB_TILE, 2 * SUB_TILE):
        if n_rows % tm == 0:
            return tm
    return SUB_TILE


def _ffn(xc, modx, modc, g_in, g_out, wg, wu, wd, *, k0, n_lat, n_rows, alias):
    b, t_all, d = xc.shape
    f = wg.shape[1]
    tm = _row_tile(n_rows)
    out_rows = t_all if alias else n_rows
    return pl.pallas_call(
        functools.partial(_ffn_kernel, tm=tm, n_lat=n_lat, k0=k0),
        grid=(b, n_rows // tm),
        in_specs=[
            pl.BlockSpec((1, tm, d), lambda i, j: (i, j, 0)),
            pl.BlockSpec((1, N_MOD, d), lambda i, j: (i, 0, 0)),
            _resident((N_MOD, d)),
            _resident((1, d)),
            _resident((1, d)),
            _resident((d, f)),
            _resident((d, f)),
            _resident((f, d)),
        ],
        out_specs=pl.BlockSpec((1, tm, d), lambda i, j: (i, j, 0)),
        out_shape=jax.ShapeDtypeStruct((b, out_rows, d), F32),
        input_output_aliases={0: 0} if alias else {},
        compiler_params=_cparams(2),
        name="ffn_half_step",
    )(xc, modx, modc, g_in, g_out, wg, wu, wd)


def _tile_lanes(t, reps):
    return t if reps == 1 else jnp.concatenate([t] * reps, axis=1)


def _rope_gqa(x, cos_t, sin_t):
    w = x.shape[1]
    reps = w // LANE
    lane = lax.broadcasted_iota(jnp.int32, x.shape, 1) % GQA_HD
    half = GQA_HD // 2
    partner = jnp.where(lane < half, pltpu.roll(x, w - half, 1), pltpu.roll(x, half, 1))
    return x * _tile_lanes(cos_t, reps) + partner * _tile_lanes(sin_t, reps)


def _rope_mla(x, cos_t, sin_t):
    w = x.shape[1]
    reps = w // LANE
    lane = lax.broadcasted_iota(jnp.int32, x.shape, 1) % HEAD_PAD
    half = MLA_ROPE // 2
    first = jnp.logical_and(lane >= MLA_NOPE, lane < MLA_NOPE + half)
    partner = jnp.where(first, pltpu.roll(x, w - half, 1), pltpu.roll(x, half, 1))
    return x * _tile_lanes(cos_t, reps) + partner * _tile_lanes(sin_t, reps)


def _head_scale(x, ms, gain):
    r = lax.rsqrt(ms + NORM_EPS)
    lane = lax.broadcasted_iota(jnp.int32, r.shape, 1)
    second = (lane >= GQA_HD).astype(jnp.int32)
    tiles = [jnp.take_along_axis(r, 2 * j + second, axis=1) for j in range(x.shape[1] // LANE)]
    return x * jnp.concatenate(tiles, axis=1) * gain


def _in_proj_kernel(x_ref, modx_ref, modc_ref, gpre_ref, wa_ref, wvg_t_ref, gkv_ref, wk_ref, wv_t_ref,
                    gq_ref, wq_ref, e_ref, ggk_ref, ggq_ref,
                    mla_cos_ref, mla_sin_ref, gqa_cos_ref, gqa_sin_ref,
                    kmla_ref, vtmla_ref, qmla_ref, kgqa_ref, vtgqa_ref, qgqa_ref, xbc_ref, dt_ref,
                    *, tm, n_lat):
    t = pl.program_id(1)
    x = x_ref[0]
    shift = _mod_rows(t, tm, n_lat, modx_ref, modc_ref, 3)
    scale = _mod_rows(t, tm, n_lat, modx_ref, modc_ref, 4)
    u = (_rms(x, gpre_ref[...]) * (1.0 + scale) + shift).astype(BF16)
    proj = lambda lo, hi: _dot(u, wa_ref[:, lo:hi])
    mla_cos = mla_cos_ref[...]
    mla_sin = mla_sin_ref[...]
    gqa_cos = gqa_cos_ref[...]
    gqa_sin = gqa_sin_ref[...]
    n_k = 2 * GQA_KV_HEADS * GQA_HD

    h_kv = proj(_A_KVLAT, _A_KDUP)
    h_q = proj(_A_QLAT, _A_QG)
    h_kd = proj(_A_KDUP, _A_XBC)
    h_qg = proj(_A_QG, _A_KR)
    h_kr = proj(_A_KR, _A_END)
    ckv = _rms(h_kv, gkv_ref[...]).astype(BF16)
    cq = _rms(h_q, gq_ref[...]).astype(BF16)
    k_nope = _dot(ckv, wk_ref[...])
    v_t = _dot_nt(wv_t_ref[...], ckv)
    q_mla = _dot(cq, wq_ref[...])
    ms_k = _dot((h_kd * h_kd).astype(BF16), e_ref[0:n_k, :])
    ms_q = _dot((h_qg * h_qg).astype(BF16), e_ref[...])
    vg_t = _dot_nt(wvg_t_ref[...], u)

    dt_ref[0] = h_kr[:, HEAD_PAD:]
    k_rope = _rope_mla(h_kr[:, 0:HEAD_PAD], mla_cos, mla_sin)
    k_mla = (k_nope + _tile_lanes(k_rope, MLA_HEADS)).astype(BF16)
    for hd in range(MLA_HEADS):
        kmla_ref[0, hd] = k_mla[:, hd * HEAD_PAD:(hd + 1) * HEAD_PAD]
    vtmla_ref[0] = v_t.astype(BF16)
    qmla_ref[0] = (_rope_mla(q_mla, mla_cos, mla_sin) * (MLA_SCALE * LOG2E)).astype(BF16)
    kd = _head_scale(h_kd, ms_k, ggk_ref[...])
    k_gqa = _rope_gqa(kd, gqa_cos, gqa_sin).astype(BF16)
    for hd in range(GQA_KV_HEADS):
        kgqa_ref[0, hd] = k_gqa[:, hd * LANE:(hd + 1) * LANE]
    qg = _head_scale(h_qg, ms_q, ggq_ref[...])
    qgqa_ref[0] = (_rope_gqa(qg, gqa_cos, gqa_sin) * (GQA_SCALE * LOG2E)).astype(BF16)
    vtgqa_ref[0] = vg_t.astype(BF16)
    xbc_ref[0] = proj(_A_XBC, _A_QLAT)


def _in_proj(xc, modx, modc, lw, tabs, *, n_lat):
    b, t_all, d = xc.shape
    tm = TOKEN_TILE
    n_qm = MLA_HEADS * HEAD_PAD
    n_kd = 2 * GQA_KV_HEADS * GQA_HD
    n_qg = GQA_HEADS * GQA_HD
    n_vm = MLA_HEADS * MLA_V
    n_vg = GQA_KV_HEADS * GQA_HD
    row = lambda w: pl.BlockSpec((1, tm, w), lambda i, j: (i, j, 0))
    col = lambda h: pl.BlockSpec((1, h, tm), lambda i, j: (i, 0, j))
    tab = pl.BlockSpec((tm, LANE), lambda i, j: (j, 0))
    res = [lw["g_pre1"], lw["wa"], lw["wvg_t"], lw["g_mla_kv"], lw["wk"], lw["wv_t"],
           lw["g_mla_q"], lw["wq"], lw["e"], lw["g_gqa_k"], lw["g_gqa_q"]]
    return pl.pallas_call(
        functools.partial(_in_proj_kernel, tm=tm, n_lat=n_lat),
        grid=(b, t_all // tm),
        in_specs=[row(d), pl.BlockSpec((1, N_MOD, d), lambda i, j: (i, 0, 0)), _resident((N_MOD, d))]
        + [_resident(w.shape) for w in res] + [tab] * 4,
        out_specs=[pl.BlockSpec((1, MLA_HEADS, tm, HEAD_PAD), lambda i, j: (i, 0, j, 0)),
                   col(n_vm), row(n_qm), pl.BlockSpec((1, GQA_KV_HEADS, tm, LANE), lambda i, j: (i, 0, j, 0)),
                   col(n_vg), row(n_qg), row(SSD_CONV_DIM), row(LANE)],
        out_shape=[
            jax.ShapeDtypeStruct((b, MLA_HEADS, t_all, HEAD_PAD), BF16),
            jax.ShapeDtypeStruct((b, n_vm, t_all), BF16),
            jax.ShapeDtypeStruct((b, t_all, n_qm), BF16),
            jax.ShapeDtypeStruct((b, GQA_KV_HEADS, t_all, LANE), BF16),
            jax.ShapeDtypeStruct((b, n_vg, t_all), BF16),
            jax.ShapeDtypeStruct((b, t_all, n_qg), BF16),
            jax.ShapeDtypeStruct((b, t_all, SSD_CONV_DIM), F32),
            jax.ShapeDtypeStruct((b, t_all, LANE), F32),
        ],
        compiler_params=_cparams(2),
        name="in_proj",
    )(xc, modx, modc, *res, *tabs)


_SSD_W = SSD_HPG * SSD_HD + 2 * SSD_STATE
_SSD_B0 = SSD_HPG * SSD_HD
_SSD_C0 = _SSD_B0 + SSD_STATE
_PAD_ROWS = 8
SSD_UNROLL = 3
_ACT_COLS = 256


def _softplus(v):
    return jnp.maximum(v, 0.0) + jnp.log1p(jnp.exp(-jnp.abs(v)))


def _ssd_kernel(xs_ref, bm_ref, cm_ref, dt_ref, wx_ref, wb_ref, wc_ref, bx_ref, bb_ref, bc_ref,
                par_ref, skip_ref, y_ref, pad_ref, act_ref, dtv_ref, e_ref, s_ref, h_ref, *, n_lat, n_ctx):
    cl = SSD_CHUNK
    t_all = n_lat + n_ctx
    n_chunks = t_all // cl
    n_ctx_chunks = n_ctx // cl
    n_lat_chunks = n_lat // cl
    g = pl.program_id(1)

    def pad_row(r):
        return pl.multiple_of(r + jnp.where(r >= n_lat, 2 * _PAD_ROWS, _PAD_ROWS), _PAD_ROWS)

    zeros = jnp.zeros((_PAD_ROWS, _SSD_W), F32)
    pad_ref[0:_PAD_ROWS, :] = zeros
    pad_ref[_PAD_ROWS + n_lat:2 * _PAD_ROWS + n_lat, :] = zeros
    pad_ref[2 * _PAD_ROWS + t_all:3 * _PAD_ROWS + t_all, :] = zeros

    def copy_body(i, carry):
        r = pl.multiple_of(i * cl, cl)
        dst = pad_row(r)
        pad_ref[pl.ds(dst, cl), 0:_SSD_B0] = xs_ref[0, pl.ds(r, cl), :]
        pad_ref[pl.ds(dst, cl), _SSD_B0:_SSD_C0] = bm_ref[0, pl.ds(r, cl), :]
        pad_ref[pl.ds(dst, cl), _SSD_C0:_SSD_W] = cm_ref[0, pl.ds(r, cl), :]
        return carry

    lax.fori_loop(0, n_chunks, copy_body, 0)

    conv_w = jnp.concatenate([wx_ref[...], wb_ref[...], wc_ref[...]], axis=1)
    conv_b = jnp.concatenate([bx_ref[...], bb_ref[...], bc_ref[...]], axis=1)
    par = par_ref[0]
    dt_bias = par[0:1, :]
    a_rate = -jnp.exp(par[1:2, :])
    lane_shift = lax.rem(LANE - 2 * SSD_HPG * g, LANE)
    half_win = SSD_CONV // 2

    def act_strand(r):
        base = pad_row(r)
        for c0 in range(0, _SSD_W, _ACT_COLS):
            cols = slice(c0, c0 + _ACT_COLS)
            blk = pad_ref[pl.ds(base - _PAD_ROWS, cl + 2 * _PAD_ROWS), cols]
            acc = conv_b[:, cols]
            for k in range(SSD_CONV):
                off = _PAD_ROWS + k - half_win
                acc = acc + conv_w[k:k + 1, cols] * blk[off:off + cl, :]
            act_ref[pl.ds(r, cl), cols] = acc * jax.nn.sigmoid(acc)
            yield
        dtv_ref[pl.ds(r, cl), :] = _softplus(pltpu.roll(dt_ref[0, pl.ds(r, cl), :], lane_shift, 1) + dt_bias)

    def group_rows(it):
        rows = [(it * SSD_UNROLL + u) * cl for u in range(SSD_UNROLL)]
        return rows if isinstance(it, int) else [pl.multiple_of(r, cl) for r in rows]

    def run_strands(strands):
        strands = list(strands)
        while strands:
            for s in list(strands):
                if next(s, "done") == "done":
                    strands.remove(s)

    n_groups = n_chunks // SSD_UNROLL
    run_strands([act_strand(r) for r in group_rows(0)])

    gw = SSD_HPG * SSD_HD
    ri = lax.broadcasted_iota(jnp.int32, (cl, cl), 0)
    ci = lax.broadcasted_iota(jnp.int32, (cl, cl), 1)
    tris = [ri >= ci, ri <= ci]
    tri_bs = [jnp.where(t, 1.0, 0.0).astype(BF16) for t in tris]
    sl = lax.broadcasted_iota(jnp.int32, (LANE, gw), 0)
    sc = lax.broadcasted_iota(jnp.int32, (LANE, gw), 1) // SSD_HD
    sels = [jnp.where(sl == sc + d * SSD_HPG, 1.0, 0.0).astype(BF16) for d in range(2)]
    skip = skip_ref[0, 0:1, :] + skip_ref[0, 1:2, :]

    def split2(v):
        hi = v.astype(BF16)
        return hi, (v - hi.astype(F32)).astype(BF16)

    def local_strand(d, r, shared, out):
        dtc = dtv_ref[pl.ds(r, cl), :]
        a_hi, a_lo = split2(dtc * a_rate)
        cs = _dot(tri_bs[d], a_hi) + _dot(tri_bs[d], a_lo)
        yield
        cs_t = cs.T
        c_hi, c_lo = split2(cs)
        dt_hi, dt_lo = split2(dtc)
        cs_w = _dot(c_hi, sels[d]) + _dot(c_lo, sels[d])
        dt_w = _dot(dt_hi, sels[d]) + _dot(dt_lo, sels[d])
        yield
        cb, bt_b = shared
        tot_w = cs_w[0:1, :] if d else cs_w[cl - 1:cl, :]
        e_ref[d, pl.ds(r, cl), :] = jnp.exp(cs_w)
        xd = act_ref[pl.ds(r, cl), 0:_SSD_B0] * dt_w
        s_ref[d, pl.ds(r, cl), :] = _dot(bt_b, (xd * jnp.exp(tot_w - cs_w)).astype(BF16))
        xd_b = xd.astype(BF16)
        yield
        first_half = lax.broadcasted_iota(jnp.int32, (cl, LANE), 1) < SSD_HD
        tiles = []
        for pair in range(SSD_HPG // 2):
            prods = []
            for j in (2 * pair, 2 * pair + 1):
                ln = d * SSD_HPG + j
                lmat = jnp.exp(jnp.where(tris[d], cs[:, ln:ln + 1] - cs_t[ln:ln + 1, :], -jnp.inf))
                prods.append(_dot((cb * lmat).astype(BF16), xd_b[:, pair * LANE:(pair + 1) * LANE]))
                yield
            tiles.append(jnp.where(first_half, prods[0], prods[1]))
        out.append(jnp.concatenate(tiles, axis=1))

    def local_group(it, prefetch):
        rows = group_rows(it)
        outs = [[] for _ in rows]
        shared = []
        for r in rows:
            bmat = act_ref[pl.ds(r, cl), _SSD_B0:_SSD_C0]
            c_b = act_ref[pl.ds(r, cl), _SSD_C0:_SSD_W].astype(BF16)
            shared.append((_dot_nt(c_b, bmat.astype(BF16)), bmat.T.astype(BF16)))
        strands = [local_strand(d, r, sh, o) for r, sh, o in zip(rows, shared, outs) for d in range(2)]
        if prefetch:
            strands += [act_strand(r) for r in group_rows(it + 1)]
        run_strands(strands)
        for r, o in zip(rows, outs):
            y_ref[0, pl.ds(r, cl), :] = skip * act_ref[pl.ds(r, cl), 0:_SSD_B0] + o[0] + o[1]

    def local_body(it, carry):
        local_group(it, True)
        return carry

    lax.fori_loop(0, n_groups - 1, local_body, 0)
    local_group(n_groups - 1, False)

    h_ref[...] = jnp.zeros(h_ref.shape, F32)

    def scan_body(it, carry):
        hs = [h_ref[0], h_ref[1]]
        todo = []
        for u in range(SSD_UNROLL):
            i = it * SSD_UNROLL + u
            c_fwd = jnp.where(i < n_ctx_chunks, n_lat_chunks + i, i - n_ctx_chunks)
            for d, c in ((0, c_fwd), (1, n_chunks - 1 - i)):
                r = pl.multiple_of(c * cl, cl)
                c_b = act_ref[pl.ds(r, cl), _SSD_C0:_SSD_W].astype(BF16)
                y_off = _dot(c_b, hs[d].astype(BF16)) * e_ref[d, pl.ds(r, cl), :]
                decay = e_ref[d, pl.ds(r + (0 if d else cl - 1), 1), :]
                hs[d] = decay * hs[d] + s_ref[d, pl.ds(r, cl), :]
                todo.append((r, y_off))
        h_ref[0] = hs[0]
        h_ref[1] = hs[1]
        for r, y_off in todo:
            y_ref[0, pl.ds(r, cl), :] = y_ref[0, pl.ds(r, cl), :] + y_off
        return carry

    lax.fori_loop(0, n_chunks // SSD_UNROLL, scan_body, 0)


def _ssd(xbc, dtr, lw, *, n_lat, n_ctx):
    b, t_all, _ = xbc.shape
    gw = SSD_HPG * SSD_HD
    assert (t_all // SSD_CHUNK) % SSD_UNROLL == 0
    n_x = SSD_INNER // SSD_STATE
    xs_blk = lambda i, g: (i, 0, g)
    b_blk = lambda i, g: (i, 0, n_x + g)
    c_blk = lambda i, g: (i, 0, n_x + SSD_GROUPS + g)
    w_cols = lambda rows: [pl.BlockSpec((rows, gw), lambda i, g: (0, g)),
                           pl.BlockSpec((rows, SSD_STATE), lambda i, g: (0, n_x + g)),
                           pl.BlockSpec((rows, SSD_STATE), lambda i, g: (0, n_x + SSD_GROUPS + g))]
    return pl.pallas_call(
        functools.partial(_ssd_kernel, n_lat=n_lat, n_ctx=n_ctx),
        grid=(b, SSD_GROUPS),
        in_specs=[
            pl.BlockSpec((1, t_all, gw), xs_blk),
            pl.BlockSpec((1, t_all, SSD_STATE), b_blk),
            pl.BlockSpec((1, t_all, SSD_STATE), c_blk),
            pl.BlockSpec((1, t_all, LANE), lambda i, g: (i, 0, 0)),
        ] + w_cols(SSD_CONV) + w_cols(1) + [
            pl.BlockSpec((1, 8, LANE), lambda i, g: (g, 0, 0)),
            pl.BlockSpec((1, 2, gw), lambda i, g: (g, 0, 0)),
        ],
        out_specs=pl.BlockSpec((1, t_all, gw), xs_blk),
        out_shape=jax.ShapeDtypeStruct((b, t_all, SSD_INNER), F32),
        scratch_shapes=[
            pltpu.VMEM((t_all + 3 * _PAD_ROWS, _SSD_W), F32),
            pltpu.VMEM((t_all, _SSD_W), F32),
            pltpu.VMEM((t_all, LANE), F32),
            pltpu.VMEM((2, t_all, gw), F32),
            pltpu.VMEM((2, t_all, gw), F32),
            pltpu.VMEM((2, SSD_STATE, gw), F32),
        ],
        compiler_params=_cparams(2),
        name="ssd_scan",
    )(xbc, xbc, xbc, dtr, lw["conv_w"], lw["conv_w"], lw["conv_w"], lw["conv_b"], lw["conv_b"],
      lw["conv_b"], lw["ssd_par"], lw["ssd_skip"])


def _attend_heads(qs, load_ks, load_vts, k_lo, k_hi):
    n = len(qs)
    tasks = [(h, lo, min(lo + KEY_CHUNK, k_hi)) for lo in range(k_lo, k_hi, KEY_CHUNK) for h in range(n)]
    nt = len(tasks)
    m, l, acc = [None] * n, [None] * n, [None] * n
    s, shift, alpha, p = {}, {}, {}, {}
    d_max, d_exp, d_pv = ATTN_STAGES
    for i in range(nt + d_pv):
        if i < nt:
            h, lo, hi = tasks[i]
            s[i] = _dot_nt(load_ks[h](lo, hi), qs[h])
        j = i - d_max
        if 0 <= j < nt:
            h = tasks[j][0]
            cm = jnp.max(s[j], axis=0, keepdims=True)
            if m[h] is None:
                alpha[j] = None
                m[h] = cm
            else:
                m_new = jnp.maximum(m[h], cm)
                alpha[j] = jnp.exp2(m[h] - m_new)
                m[h] = m_new
            shift[j] = m[h]
        j = i - d_exp
        if 0 <= j < nt:
            p[j] = jnp.exp2(s.pop(j) - shift.pop(j)).astype(BF16)
        j = i - d_pv
        if 0 <= j < nt:
            h, lo, hi = tasks[j]
            v_ext = jnp.concatenate([load_vts[h](lo, hi), jnp.ones((SUM_ROWS, hi - lo), BF16)], axis=0)
            pv = _dot(v_ext, p.pop(j))
            a = alpha.pop(j)
            acc[h] = pv if a is None else a * acc[h] + pv
    dv = acc[0].shape[0] - SUM_ROWS
    return [acc[h][0:dv, :] / acc[h][dv:dv + 1, :] for h in range(n)]


def _by_query_tile(compute, n_lat, n_ctx, nq_ctx, with_ctx_queries):
    if not with_ctx_queries:
        compute(0, n_lat + n_ctx)
        return
    qi = pl.program_id(2)

    @pl.when(qi >= nq_ctx)
    def _():
        compute(0, n_lat + n_ctx)

    @pl.when(qi < nq_ctx)
    def _():
        compute(n_lat, n_lat + n_ctx)


def _store_heads(o_ref, outs):
    for i in range(0, len(outs), 2):
        pair = jnp.concatenate(outs[i:i + 2], axis=0).T.astype(BF16)
        o_ref[0, :, i // 2 * LANE:(i // 2 + 1) * LANE] = pair


def _mla_attn_kernel(q_ref, k_ref, vt_ref, o_ref, *, n_lat, n_ctx, nq_ctx, with_ctx_queries):
    def compute(k_lo, k_hi):
        lanes = [slice(i * HEAD_PAD, (i + 1) * HEAD_PAD) for i in range(ATTN_HEADS)]
        rows = [slice(i * MLA_V, (i + 1) * MLA_V) for i in range(ATTN_HEADS)]
        outs = _attend_heads([q_ref[0, :, ln] for ln in lanes],
                             [lambda lo, hi, i=i: k_ref[0, i, lo:hi, :] for i in range(ATTN_HEADS)],
                             [lambda lo, hi, rw=rw: vt_ref[0, rw, lo:hi] for rw in rows], k_lo, k_hi)
        _store_heads(o_ref, outs)

    _by_query_tile(compute, n_lat, n_ctx, nq_ctx, with_ctx_queries)


def _gqa_attn_kernel(q_ref, k_ref, vt_ref, o_ref, *, n_lat, n_ctx, nq_ctx, with_ctx_queries):
    def compute(k_lo, k_hi):
        qs = []
        for i in range(ATTN_HEADS):
            q2 = q_ref[0, :, i // 2 * LANE:(i // 2 + 1) * LANE]
            lane = lax.broadcasted_iota(jnp.int32, q2.shape, 1)
            sel = (lane < GQA_HD) if i % 2 == 0 else (lane >= GQA_HD)
            qs.append(jnp.where(sel, q2, jnp.zeros_like(q2)))
        rep = GQA_HEADS // GQA_KV_HEADS
        v_rows = [slice(i // rep * GQA_HD, (i // rep + 1) * GQA_HD) for i in range(ATTN_HEADS)]
        outs = _attend_heads(qs, [lambda lo, hi, i=i: k_ref[0, i // rep, lo:hi, :] for i in range(ATTN_HEADS)],
                             [lambda lo, hi, rw=rw: vt_ref[0, rw, lo:hi] for rw in v_rows], k_lo, k_hi)
        _store_heads(o_ref, outs)

    _by_query_tile(compute, n_lat, n_ctx, nq_ctx, with_ctx_queries)


def _attention(kind, q, k, v_t, *, n_lat, n_ctx, with_ctx_queries):
    b, t_all, _ = q.shape
    tq = TOKEN_TILE
    nq_lat = n_lat // tq
    nq = t_all // tq if with_ctx_queries else nq_lat
    q_tile = lambda j: (j + nq_lat) % nq
    nh = ATTN_HEADS
    rep = GQA_HEADS // GQA_KV_HEADS
    assert nh % rep == 0 and MLA_HEADS % nh == 0
    if kind == "mla":
        body = _mla_attn_kernel
        in_specs = [
            pl.BlockSpec((1, tq, nh * HEAD_PAD), lambda i, p, j: (i, q_tile(j), p)),
            pl.BlockSpec((1, nh, t_all, HEAD_PAD), lambda i, p, j: (i, p, 0, 0)),
            pl.BlockSpec((1, nh * MLA_V, t_all), lambda i, p, j: (i, p, 0)),
        ]
    else:
        body = _gqa_attn_kernel
        in_specs = [
            pl.BlockSpec((1, tq, nh * GQA_HD), lambda i, p, j: (i, q_tile(j), p)),
            pl.BlockSpec((1, nh // rep, t_all, 2 * GQA_HD), lambda i, p, j: (i, p, 0, 0)),
            pl.BlockSpec((1, nh // rep * GQA_HD, t_all), lambda i, p, j: (i, p, 0)),
        ]
    return pl.pallas_call(
        functools.partial(body, n_lat=n_lat, n_ctx=n_ctx, nq_ctx=nq - nq_lat, with_ctx_queries=with_ctx_queries),
        grid=(b, MLA_HEADS // nh, nq),
        in_specs=in_specs,
        out_specs=pl.BlockSpec((1, tq, nh * MLA_V), lambda i, p, j: (i, q_tile(j), p)),
        out_shape=jax.ShapeDtypeStruct((b, t_all, MLA_HEADS * MLA_V), BF16),
        compiler_params=_cparams(3),
        name=kind + "_attention",
    )(q, k, v_t)


def _out_kernel(x_ref, modx_ref, modc_ref, gpre_ref, gpost_ref, omla_ref, ogqa_ref, y_ref, wzg_ref, gssd_ref,
                wmo_ref, wgo_ref, wso_ref, wout_ref, o_ref, *, tm, n_lat):
    t = pl.program_id(1)
    d = x_ref.shape[2]
    gate_cols = lambda i: slice(SSD_INNER + i * d, SSD_INNER + (i + 1) * d)

    def strand(r0):
        rows = slice(r0, r0 + OUT_SUB)
        mod = lambda k: _mod_rows(t, tm, n_lat, modx_ref, modc_ref, k, r0, OUT_SUB)
        x = x_ref[0, rows, :]
        u = (_rms(x, gpre_ref[...]) * (1.0 + mod(4)) + mod(3)).astype(BF16)
        yield
        z = _dot(u, wzg_ref[:, 0:SSD_INNER])
        a_mla = _dot(omla_ref[0, rows, :], wmo_ref[...])
        g0 = _dot(u, wzg_ref[:, gate_cols(0)])
        yield
        a_gqa = _dot(ogqa_ref[0, rows, :], wgo_ref[...])
        g1 = _dot(u, wzg_ref[:, gate_cols(1)])
        y = _rms(y_ref[0, rows, :] * (z * jax.nn.sigmoid(z)), gssd_ref[...]).astype(BF16)
        yield
        g2 = _dot(u, wzg_ref[:, gate_cols(2)])
        a_ssd = _dot(y, wso_ref[...])
        merged = jax.nn.sigmoid(g0) * a_mla + jax.nn.sigmoid(g1) * a_gqa
        yield
        merged = (merged + jax.nn.sigmoid(g2) * a_ssd).astype(BF16)
        out = _dot(merged, wout_ref[...])
        yield
        o_ref[0, rows, :] = x + mod(5) * _rms(out, gpost_ref[...])

    _emit_skewed([strand(r0) for r0 in range(0, tm, OUT_SUB)])


def _mixer_out(xc, modx, modc, omla, ogqa, y, lw, *, n_lat, n_rows):
    b, t_all, d = xc.shape
    tm = OUT_SUB
    row = lambda w: pl.BlockSpec((1, tm, w), lambda i, j: (i, j, 0))
    res = [lw["wzg"], lw["g_ssd"], lw["w_mla_o"], lw["w_gqa_o"], lw["w_ssd_o"], lw["w_out"]]
    return pl.pallas_call(
        functools.partial(_out_kernel, tm=tm, n_lat=n_lat),
        grid=(b, n_rows // tm),
        in_specs=[row(d), pl.BlockSpec((1, N_MOD, d), lambda i, j: (i, 0, 0)), _resident((N_MOD, d)),
                  _resident((1, d)), _resident((1, d)),
                  row(MLA_HEADS * MLA_V), row(GQA_HEADS * GQA_HD), row(SSD_INNER)]
        + [_resident(w.shape) for w in res],
        out_specs=row(d),
        out_shape=jax.ShapeDtypeStruct((b, t_all, d), F32),
        input_output_aliases={0: 0},
        compiler_params=_cparams(2),
        name="mixer_out",
    )(xc, modx, modc, lw["g_pre1"], lw["g_post1"], omla, ogqa, y, *res)


def _rope_tables(n_lat, n_ctx):
    rows = n_lat // GRID_W
    r = jnp.repeat(jnp.arange(rows, dtype=F32), GRID_W)
    c = jnp.tile(jnp.arange(GRID_W, dtype=F32), rows)

    def angles(rot_dim):
        n_freq = rot_dim // 4
        inv = ROPE_THETA ** (-jnp.arange(n_freq, dtype=F32) / n_freq)
        return jnp.concatenate([r[:, None] * inv, c[:, None] * inv], axis=-1)

    def finish(cos_l, sin_l):
        ident_c = jnp.ones((n_ctx, LANE), F32)
        ident_s = jnp.zeros((n_ctx, LANE), F32)
        return jnp.concatenate([cos_l, ident_c], 0), jnp.concatenate([sin_l, ident_s], 0)

    ang = angles(MLA_ROPE)
    cos, sin = jnp.cos(ang), jnp.sin(ang)
    ones = jnp.ones((n_lat, MLA_NOPE), F32)
    zeros = jnp.zeros((n_lat, MLA_NOPE), F32)
    tail1 = jnp.ones((n_lat, HEAD_PAD - MLA_NOPE - MLA_ROPE), F32)
    tail0 = jnp.zeros((n_lat, HEAD_PAD - MLA_NOPE - MLA_ROPE), F32)
    mla = finish(jnp.concatenate([ones, cos, cos, tail1], 1), jnp.concatenate([zeros, -sin, sin, tail0], 1))

    ang = angles(GQA_HD)
    cos, sin = jnp.cos(ang), jnp.sin(ang)
    gqa = finish(jnp.concatenate([cos, cos, cos, cos], 1), jnp.concatenate([-sin, sin, -sin, sin], 1))
    return mla + gqa


def _head_indicator():
    head = jnp.arange(GQA_HEADS * GQA_HD) // GQA_HD
    e = (head[:, None] == jnp.arange(LANE)[None, :])
    return (e.astype(F32) / GQA_HD).astype(BF16)


def _group_lanes(v):
    return v.reshape(2, SSD_GROUPS, SSD_HPG).transpose(1, 0, 2).reshape(SSD_GROUPS, 2 * SSD_HPG)


def _layer_weights(l, p):
    d = p["w_in"].shape[1]
    w = p["w_in"][l]
    kg = w[:, _C_KG:_C_VG].reshape(d, GQA_KV_HEADS, GQA_HD)
    kdup = jnp.concatenate([kg, kg], axis=-1).reshape(d, 2 * GQA_KV_HEADS * GQA_HD)
    kr = jnp.zeros((d, HEAD_PAD), F32).at[:, MLA_NOPE:MLA_NOPE + MLA_ROPE].set(w[:, _C_KROPE:_C_KG])
    wdt = w[:, _C_DT:_C_QLAT].reshape(d, 2, SSD_GROUPS, SSD_HPG).transpose(0, 2, 1, 3).reshape(d, 2 * SSD_HEADS)
    wdt = jnp.pad(wdt, ((0, 0), (0, LANE - 2 * SSD_HEADS)))
    wa = jnp.concatenate([w[:, _C_KVLAT:_C_KROPE], kdup, w[:, _C_XBC:_C_DT], w[:, _C_QLAT:_C_QG],
                          w[:, _C_QG:_C_Z], kr, wdt], axis=1).astype(BF16)

    wkv = p["w_mla_kv_up"][l].reshape(MLA_KV_RANK, MLA_HEADS, MLA_NOPE + MLA_V)
    wk = jnp.pad(wkv[:, :, :MLA_NOPE], ((0, 0), (0, 0), (0, HEAD_PAD - MLA_NOPE)))
    wk = wk.reshape(MLA_KV_RANK, MLA_HEADS * HEAD_PAD).astype(BF16)
    wv_t = wkv[:, :, MLA_NOPE:].reshape(MLA_KV_RANK, MLA_HEADS * MLA_V).T.astype(BF16)
    wq = p["w_mla_q_up"][l].reshape(MLA_Q_RANK, MLA_HEADS, MLA_NOPE + MLA_ROPE)
    wq = jnp.pad(wq, ((0, 0), (0, 0), (0, HEAD_PAD - MLA_NOPE - MLA_ROPE)))
    wq = wq.reshape(MLA_Q_RANK, MLA_HEADS * HEAD_PAD).astype(BF16)

    par = jnp.zeros((SSD_GROUPS, 8, LANE), F32)
    par = par.at[:, 0, :2 * SSD_HPG].set(_group_lanes(p["dt_bias"][l]))
    par = par.at[:, 1, :2 * SSD_HPG].set(_group_lanes(p["a_log"][l]))
    skip = jnp.repeat(p["ssd_skip"][l].reshape(2, SSD_GROUPS, SSD_HPG), SSD_HD, axis=2).transpose(1, 0, 2)

    row = lambda v: v.reshape(1, -1).astype(F32)
    ffn = lambda s: (p["w_ffn_gate"][l, s].astype(BF16), p["w_ffn_up"][l, s].astype(BF16),
                     p["w_ffn_down"][l, s].astype(BF16))
    return {
        "ffn0": ffn(0), "ffn1": ffn(1),
        "g_pre0": row(p["g_pre"][l, 0]), "g_pre1": row(p["g_pre"][l, 1]), "g_pre2": row(p["g_pre"][l, 2]),
        "g_post0": row(p["g_post"][l, 0]), "g_post1": row(p["g_post"][l, 1]), "g_post2": row(p["g_post"][l, 2]),
        "wa": wa, "wvg_t": w[:, _C_VG:_C_XBC].T.astype(BF16),
        "g_mla_kv": row(p["g_mla_kv"][l]), "wk": wk, "wv_t": wv_t,
        "g_mla_q": row(p["g_mla_q"][l]), "wq": wq, "e": _head_indicator(),
        "g_gqa_k": row(jnp.tile(p["g_gqa_k"][l], 2 * GQA_KV_HEADS)),
        "g_gqa_q": row(jnp.tile(p["g_gqa_q"][l], GQA_HEADS)),
        "conv_w": p["conv_w"][l], "conv_b": row(p["conv_b"][l]), "ssd_par": par, "ssd_skip": skip,
        "wzg": w[:, _C_Z:].astype(BF16), "g_ssd": row(p["g_ssd"][l]),
        "w_mla_o": p["w_mla_o"][l].astype(BF16), "w_gqa_o": p["w_gqa_o"][l].astype(BF16),
        "w_ssd_o": p["w_ssd_o"][l].astype(BF16), "w_out": p["w_out"][l].astype(BF16),
    }


def kernel(x, c, ctx, c_ctx, w_mod, b_mod, g_pre, g_post, w_ffn_gate, w_ffn_up, w_ffn_down, w_in, g_mla_q, w_mla_q_up, g_mla_kv, w_mla_kv_up, g_gqa_q, g_gqa_k, conv_w, conv_b, dt_bias, a_log, ssd_skip, g_ssd, w_mla_o, w_gqa_o, w_ssd_o, w_out):
    b, n_lat, d = x.shape
    n_ctx = ctx.shape[1]
    t_all = n_lat + n_ctx
    depth = w_in.shape[0]
    assert n_lat % TOKEN_TILE == 0 and n_ctx % TOKEN_TILE == 0 and n_lat % GRID_W == 0
    assert w_in.shape[2] == _C_GATE + 3 * d and d == SSD_INNER
    p = dict(g_pre=g_pre, g_post=g_post, w_ffn_gate=w_ffn_gate, w_ffn_up=w_ffn_up, w_ffn_down=w_ffn_down,
             w_in=w_in, g_mla_q=g_mla_q, w_mla_q_up=w_mla_q_up, g_mla_kv=g_mla_kv, w_mla_kv_up=w_mla_kv_up,
             g_gqa_q=g_gqa_q, g_gqa_k=g_gqa_k, conv_w=conv_w, conv_b=conv_b, dt_bias=dt_bias, a_log=a_log,
             ssd_skip=ssd_skip, g_ssd=g_ssd, w_mla_o=w_mla_o, w_gqa_o=w_gqa_o, w_ssd_o=w_ssd_o, w_out=w_out)

    rows = -(-(b + 1) // 8) * 8
    c_all = jnp.concatenate([c, c_ctx[None, :], jnp.zeros((rows - b - 1, d), F32)], axis=0)
    mod = _modulation(c_all, w_mod, b_mod).reshape(depth, rows, N_MOD, d)
    tabs = _rope_tables(n_lat, n_ctx)
    xc = jnp.concatenate([x, ctx], axis=1)

    for l in range(depth):
        last = l == depth - 1
        lw = _layer_weights(l, p)
        modx, modc = mod[l, :b], mod[l, b]
        xc = _ffn(xc, modx, modc, lw["g_pre0"], lw["g_post0"], *lw["ffn0"],
                  k0=0, n_lat=n_lat, n_rows=t_all, alias=True)
        kmla, vtmla, qmla, kgqa, vtgqa, qgqa, xbc, dtr = _in_proj(xc, modx, modc, lw, tabs, n_lat=n_lat)
        y = _ssd(xbc, dtr, lw, n_lat=n_lat, n_ctx=n_ctx)
        omla = _attention("mla", qmla, kmla, vtmla, n_lat=n_lat, n_ctx=n_ctx, with_ctx_queries=not last)
        ogqa = _attention("gqa", qgqa, kgqa, vtgqa, n_lat=n_lat, n_ctx=n_ctx, with_ctx_queries=not last)
        n_rows = n_lat if last else t_all
        xc = _mixer_out(xc, modx, modc, omla, ogqa, y, lw, n_lat=n_lat, n_rows=n_rows)
        xc = _ffn(xc, modx, modc, lw["g_pre2"], lw["g_post2"], *lw["ffn1"],
                  k0=6, n_lat=n_lat, n_rows=n_rows, alias=not last)
    return xc
```

```python
import functools
import math

import jax
import jax.numpy as jnp
from jax import lax
from jax.experimental import pallas as pl
from jax.experimental.pallas import tpu as pltpu

F32 = jnp.float32
BF16 = jnp.bfloat16

GRID_W = 64
ROPE_THETA = 10000.0
NORM_EPS = 1e-6
FFN_RES = 0.5
N_MOD = 9

MLA_HEADS = 16
MLA_NOPE = 64
MLA_ROPE = 32
MLA_V = 64
MLA_Q_RANK = 512
MLA_KV_RANK = 256
GQA_HEADS = 16
GQA_KV_HEADS = 4
GQA_HD = 64
SSD_INNER = 1024
SSD_HD = 64
SSD_HEADS = 16
SSD_GROUPS = 4
SSD_HPG = 4
SSD_STATE = 128
SSD_CONV = 5
SSD_CHUNK = 128
SSD_CONV_DIM = SSD_INNER + 2 * SSD_GROUPS * SSD_STATE
MLA_SCALE = (MLA_NOPE + MLA_ROPE) ** -0.5
GQA_SCALE = GQA_HD ** -0.5
LOG2E = math.log2(math.e)

LANE = 128
HEAD_PAD = 128
TOKEN_TILE = 256
SUB_TILE = 256
OUT_SUB = 256
FFN_COLS = 768
KEY_CHUNK = 256
ATTN_HEADS = 16
SUM_ROWS = 16
ATTN_STAGES = (4, 8, 12)
VMEM_LIMIT = 56 * 1024 * 1024

_C_KVLAT = 0
_C_KROPE = _C_KVLAT + MLA_KV_RANK
_C_KG = _C_KROPE + MLA_ROPE
_C_VG = _C_KG + GQA_KV_HEADS * GQA_HD
_C_XBC = _C_VG + GQA_KV_HEADS * GQA_HD
_C_DT = _C_XBC + SSD_CONV_DIM
_C_QLAT = _C_DT + 2 * SSD_HEADS
_C_QG = _C_QLAT + MLA_Q_RANK
_C_Z = _C_QG + GQA_HEADS * GQA_HD
_C_GATE = _C_Z + SSD_INNER

_A_KVLAT = 0
_A_KDUP = _A_KVLAT + MLA_KV_RANK
_A_XBC = _A_KDUP + 2 * GQA_KV_HEADS * GQA_HD
_A_QLAT = _A_XBC + SSD_CONV_DIM
_A_QG = _A_QLAT + MLA_Q_RANK
_A_KR = _A_QG + GQA_HEADS * GQA_HD
_A_DT = _A_KR + HEAD_PAD
_A_END = _A_DT + LANE


def _cparams(n_axes):
    return pltpu.CompilerParams(
        dimension_semantics=("parallel",) * n_axes, vmem_limit_bytes=VMEM_LIMIT)


def _resident(shape):
    nd = len(shape)
    return pl.BlockSpec(shape, lambda *_: (0,) * nd, pipeline_mode=pl.Buffered(1))


def _rms(t, g):
    return t * lax.rsqrt(jnp.mean(t * t, axis=-1, keepdims=True) + NORM_EPS) * g


def _dot(a, b):
    return jnp.dot(a, b, preferred_element_type=F32)


def _dot_nt(a, b):
    return lax.dot_general(a, b, (((1,), (1,)), ((), ())), preferred_element_type=F32)


def _mod_rows(tile_idx, tm, n_lat, modx_ref, modc_ref, k, row0=0, n_rows=None):
    n_rows = tm if n_rows is None else n_rows
    rows = tile_idx * tm + row0 + lax.broadcasted_iota(jnp.int32, (n_rows, 1), 0)
    return jnp.where(rows >= n_lat, modc_ref[k:k + 1, :], modx_ref[0, k:k + 1, :])


def _emit_skewed(strands):
    pending, active = list(strands), []
    while pending or active:
        if pending:
            active.append(pending.pop(0))
        for s in list(active):
            if next(s, "done") == "done":
                active.remove(s)


def _mod_kernel(c_ref, w_ref, b_ref, o_ref):
    s = c_ref[...]
    s = s * jax.nn.sigmoid(s)
    o_ref[0] = jnp.dot(s, w_ref[0], preferred_element_type=F32,
                       precision=lax.Precision.HIGHEST) + b_ref[0]


def _modulation(c_all, w_mod, b_mod):
    n_layers, d, _ = w_mod.shape
    rows = c_all.shape[0]
    return pl.pallas_call(
        _mod_kernel,
        grid=(n_layers, N_MOD),
        in_specs=[
            pl.BlockSpec((rows, d), lambda l, j: (0, 0)),
            pl.BlockSpec((1, d, d), lambda l, j: (l, 0, j)),
            pl.BlockSpec((1, 1, d), lambda l, j: (l, 0, j)),
        ],
        out_specs=pl.BlockSpec((1, rows, d), lambda l, j: (l, 0, j)),
        out_shape=jax.ShapeDtypeStruct((n_layers, rows, N_MOD * d), F32),
        compiler_params=_cparams(2),
        name="modulation",
    )(c_all, w_mod, b_mod.reshape(n_layers, 1, N_MOD * d))


def _ffn_kernel(x_ref, modx_ref, modc_ref, gin_ref, gout_ref, wg_ref, wu_ref, wd_ref, o_ref,
                *, tm, n_lat, k0):
    t = pl.program_id(1)
    f = wg_ref.shape[1]
    cuts = list(range(0, f, FFN_COLS)) + [f]

    def strand(r0):
        rows = slice(r0, r0 + SUB_TILE)
        mod = lambda k: _mod_rows(t, tm, n_lat, modx_ref, modc_ref, k, r0, SUB_TILE)
        x = x_ref[0, rows, :]
        u = (_rms(x, gin_ref[...]) * (1.0 + mod(k0 + 1)) + mod(k0)).astype(BF16)
        yield
        hidden = []
        for lo, hi in zip(cuts[:-1], cuts[1:]):
            hidden.append((_dot(u, wg_ref[:, lo:hi]), _dot(u, wu_ref[:, lo:hi])))
            yield
        y = None
        for (hg, hu), lo, hi in zip(hidden, cuts[:-1], cuts[1:]):
            a = (hg * jax.nn.sigmoid(hg) * hu).astype(BF16)
            part = _dot(a, wd_ref[lo:hi, :])
            y = part if y is None else y + part
            yield
        o_ref[0, rows, :] = x + (FFN_RES * mod(k0 + 2)) * _rms(y, gout_ref[...])

    _emit_skewed([strand(r0) for r0 in range(0, tm, SUB_TILE)])


def _row_tile(n_rows):
    for tm in (3 * SUB_TILE, 2 * SUB_TILE):
        if n_rows % tm == 0:
            return tm
    return SUB_TILE


def _ffn(xc, modx, modc, g_in, g_out, wg, wu, wd, *, k0, n_lat, n_rows, alias):
    b, t_all, d = xc.shape
    f = wg.shape[1]
    tm = _row_tile(n_rows)
    out_rows = t_all if alias else n_rows
    return pl.pallas_call(
        functools.partial(_ffn_kernel, tm=tm, n_lat=n_lat, k0=k0),
        grid=(b, n_rows // tm),
        in_specs=[
            pl.BlockSpec((1, tm, d), lambda i, j: (i, j, 0)),
            pl.BlockSpec((1, N_MOD, d), lambda i, j: (i, 0, 0)),
            _resident((N_MOD, d)),
            _resident((1, d)),
            _resident((1, d)),
            _resident((d, f)),
            _resident((d, f)),
            _resident((f, d)),
        ],
        out_specs=pl.BlockSpec((1, tm, d), lambda i, j: (i, j, 0)),
        out_shape=jax.ShapeDtypeStruct((b, out_rows, d), F32),
        input_output_aliases={0: 0} if alias else {},
        compiler_params=_cparams(2),
        name="ffn_half_step",
    )(xc, modx, modc, g_in, g_out, wg, wu, wd)


def _tile_lanes(t, reps):
    return t if reps == 1 else jnp.concatenate([t] * reps, axis=1)


def _rope_gqa(x, cos_t, sin_t):
    w = x.shape[1]
    reps = w // LANE
    lane = lax.broadcasted_iota(jnp.int32, x.shape, 1) % GQA_HD
    half = GQA_HD // 2
    partner = jnp.where(lane < half, pltpu.roll(x, w - half, 1), pltpu.roll(x, half, 1))
    return x * _tile_lanes(cos_t, reps) + partner * _tile_lanes(sin_t, reps)


def _rope_mla(x, cos_t, sin_t):
    w = x.shape[1]
    partner = pltpu.roll(x, w - MLA_ROPE, 1)
    return x * _tile_lanes(cos_t, w // LANE) + partner * _tile_lanes(sin_t, w // LANE)


def _head_scale(x, ms, gain):
    r = lax.rsqrt(ms + NORM_EPS)
    lane = lax.broadcasted_iota(jnp.int32, r.shape, 1)
    second = (lane >= GQA_HD).astype(jnp.int32)
    tiles = [jnp.take_along_axis(r, 2 * j + second, axis=1) for j in range(x.shape[1] // LANE)]
    return x * jnp.concatenate(tiles, axis=1) * gain


def _in_proj_kernel(x_ref, modx_ref, modc_ref, gpre_ref, wa_ref, wvg_t_ref, gkv_ref, wk_ref, wv_t_ref,
                    gq_ref, wq_ref, e_ref, ggk_ref, ggq_ref,
                    mla_cos_ref, mla_sin_ref, gqa_cos_ref, gqa_sin_ref,
                    kmla_ref, vtmla_ref, qmla_ref, kgqa_ref, vtgqa_ref, qgqa_ref, xbc_ref, dt_ref,
                    *, tm, n_lat):
    t = pl.program_id(1)
    x = x_ref[0]
    shift = _mod_rows(t, tm, n_lat, modx_ref, modc_ref, 3)
    scale = _mod_rows(t, tm, n_lat, modx_ref, modc_ref, 4)
    u = (_rms(x, gpre_ref[...]) * (1.0 + scale) + shift).astype(BF16)
    proj = lambda lo, hi: _dot(u, wa_ref[:, lo:hi])
    mla_cos = mla_cos_ref[...]
    mla_sin = mla_sin_ref[...]
    gqa_cos = gqa_cos_ref[...]
    gqa_sin = gqa_sin_ref[...]
    n_k = 2 * GQA_KV_HEADS * GQA_HD

    h_q = proj(_A_QLAT, _A_QG)
    h_qg = proj(_A_QG, _A_KR)
    cq = _rms(h_q, gq_ref[...]).astype(BF16)
    q_mla = _dot(cq, wq_ref[...])
    ms_q = _dot((h_qg * h_qg).astype(BF16), e_ref[...])
    h_kv = proj(_A_KVLAT, _A_KDUP)
    h_kd = proj(_A_KDUP, _A_XBC)
    h_kr = proj(_A_KR, _A_END)
    qmla_ref[0] = (_rope_mla(q_mla, mla_cos, mla_sin) * (MLA_SCALE * LOG2E)).astype(BF16)
    ckv = _rms(h_kv, gkv_ref[...]).astype(BF16)
    k_nope = _dot(ckv, wk_ref[...])
    v_t = _dot_nt(wv_t_ref[...], ckv)
    ms_k = _dot((h_kd * h_kd).astype(BF16), e_ref[0:n_k, :])
    vg_t = _dot_nt(wvg_t_ref[...], u)
    qg = _head_scale(h_qg, ms_q, ggq_ref[...])
    qgqa_ref[0] = (_rope_gqa(qg, gqa_cos, gqa_sin) * (GQA_SCALE * LOG2E)).astype(BF16)

    dt_ref[0] = h_kr[:, HEAD_PAD:]
    k_rope = _rope_mla(h_kr[:, 0:HEAD_PAD], mla_cos, mla_sin)
    k_rope = jnp.where(lax.broadcasted_iota(jnp.int32, k_rope.shape, 1) < MLA_NOPE + MLA_ROPE, k_rope, 0.0)
    k_mla =(k_nope + _tile_lanes(k_rope, MLA_HEADS)).astype(BF16)
    for hd in range(MLA_HEADS):
        kmla_ref[0, hd] = k_mla[:, hd * HEAD_PAD:(hd + 1) * HEAD_PAD]
    vtmla_ref[0] = v_t.astype(BF16)
    kd = _head_scale(h_kd, ms_k, ggk_ref[...])
    k_gqa = _rope_gqa(kd, gqa_cos, gqa_sin).astype(BF16)
    for hd in range(GQA_KV_HEADS):
        kgqa_ref[0, hd] = k_gqa[:, hd * LANE:(hd + 1) * LANE]
    vtgqa_ref[0] = vg_t.astype(BF16)
    xbc_ref[0] = proj(_A_XBC, _A_QLAT)


def _in_proj(xc, modx, modc, lw, tabs, *, n_lat):
    b, t_all, d = xc.shape
    tm = TOKEN_TILE
    n_qm = MLA_HEADS * HEAD_PAD
    n_kd = 2 * GQA_KV_HEADS * GQA_HD
    n_qg = GQA_HEADS * GQA_HD
    n_vm = MLA_HEADS * MLA_V
    n_vg = GQA_KV_HEADS * GQA_HD
    row = lambda w: pl.BlockSpec((1, tm, w), lambda i, j: (i, j, 0))
    col = lambda h: pl.BlockSpec((1, h, tm), lambda i, j: (i, 0, j))
    tab = pl.BlockSpec((tm, LANE), lambda i, j: (j, 0))
    res = [lw["g_pre1"], lw["wa"], lw["wvg_t"], lw["g_mla_kv"], lw["wk"], lw["wv_t"],
           lw["g_mla_q"], lw["wq"], lw["e"], lw["g_gqa_k"], lw["g_gqa_q"]]
    return pl.pallas_call(
        functools.partial(_in_proj_kernel, tm=tm, n_lat=n_lat),
        grid=(b, t_all // tm),
        in_specs=[row(d), pl.BlockSpec((1, N_MOD, d), lambda i, j: (i, 0, 0)), _resident((N_MOD, d))]
        + [_resident(w.shape) for w in res] + [tab] * 4,
        out_specs=[pl.BlockSpec((1, MLA_HEADS, tm, HEAD_PAD), lambda i, j: (i, 0, j, 0)),
                   col(n_vm), row(n_qm), pl.BlockSpec((1, GQA_KV_HEADS, tm, LANE), lambda i, j: (i, 0, j, 0)),
                   col(n_vg), row(n_qg), row(SSD_CONV_DIM), row(LANE)],
        out_shape=[
            jax.ShapeDtypeStruct((b, MLA_HEADS, t_all, HEAD_PAD), BF16),
            jax.ShapeDtypeStruct((b, n_vm, t_all), BF16),
            jax.ShapeDtypeStruct((b, t_all, n_qm), BF16),
            jax.ShapeDtypeStruct((b, GQA_KV_HEADS, t_all, LANE), BF16),
            jax.ShapeDtypeStruct((b, n_vg, t_all), BF16),
            jax.ShapeDtypeStruct((b, t_all, n_qg), BF16),
            jax.ShapeDtypeStruct((b, t_all, SSD_CONV_DIM), F32),
            jax.ShapeDtypeStruct((b, t_all, LANE), F32),
        ],
        compiler_params=_cparams(2),
        name="in_proj",
    )(xc, modx, modc, *res, *tabs)


_SSD_W = SSD_HPG * SSD_HD + 2 * SSD_STATE
_SSD_B0 = SSD_HPG * SSD_HD
_SSD_C0 = _SSD_B0 + SSD_STATE
_PAD_ROWS = 8
SSD_UNROLL = 3
_ACT_COLS = 256


def _softplus(v):
    return jnp.maximum(v, 0.0) + jnp.log1p(jnp.exp(-jnp.abs(v)))


def _ssd_kernel(xs_ref, bm_ref, cm_ref, dt_ref, wx_ref, wb_ref, wc_ref, bx_ref, bb_ref, bc_ref,
                par_ref, skip_ref, y_ref, pad_ref, act_ref, dtv_ref, e_ref, s_ref, h_ref, *, n_lat, n_ctx):
    cl = SSD_CHUNK
    t_all = n_lat + n_ctx
    n_chunks = t_all // cl
    n_ctx_chunks = n_ctx // cl
    n_lat_chunks = n_lat // cl
    g = pl.program_id(1)

    def pad_row(r):
        return pl.multiple_of(r + jnp.where(r >= n_lat, 2 * _PAD_ROWS, _PAD_ROWS), _PAD_ROWS)

    zeros = jnp.zeros((_PAD_ROWS, _SSD_W), F32)
    pad_ref[0:_PAD_ROWS, :] = zeros
    pad_ref[_PAD_ROWS + n_lat:2 * _PAD_ROWS + n_lat, :] = zeros
    pad_ref[2 * _PAD_ROWS + t_all:3 * _PAD_ROWS + t_all, :] = zeros

    def copy_body(i, carry):
        r = pl.multiple_of(i * cl, cl)
        dst = pad_row(r)
        pad_ref[pl.ds(dst, cl), 0:_SSD_B0] = xs_ref[0, pl.ds(r, cl), :]
        pad_ref[pl.ds(dst, cl), _SSD_B0:_SSD_C0] = bm_ref[0, pl.ds(r, cl), :]
        pad_ref[pl.ds(dst, cl), _SSD_C0:_SSD_W] = cm_ref[0, pl.ds(r, cl), :]
        return carry

    lax.fori_loop(0, n_chunks, copy_body, 0)

    conv_w = jnp.concatenate([wx_ref[...], wb_ref[...], wc_ref[...]], axis=1)
    conv_b = jnp.concatenate([bx_ref[...], bb_ref[...], bc_ref[...]], axis=1)
    par = par_ref[0]
    dt_bias = par[0:1, :]
    a_rate = -jnp.exp(par[1:2, :])
    lane_shift = lax.rem(LANE - 2 * SSD_HPG * g, LANE)
    half_win = SSD_CONV // 2

    def act_strand(r):
        base = pad_row(r)
        for c0 in range(0, _SSD_W, _ACT_COLS):
            cols = slice(c0, c0 + _ACT_COLS)
            blk = pad_ref[pl.ds(base - _PAD_ROWS, cl + 2 * _PAD_ROWS), cols]
            acc = conv_b[:, cols]
            for k in range(SSD_CONV):
                off = _PAD_ROWS + k - half_win
                acc = acc + conv_w[k:k + 1, cols] * blk[off:off + cl, :]
            act_ref[pl.ds(r, cl), cols] = acc * jax.nn.sigmoid(acc)
            yield
        dtv_ref[pl.ds(r, cl), :] = _softplus(pltpu.roll(dt_ref[0, pl.ds(r, cl), :], lane_shift, 1) + dt_bias)

    def group_rows(it):
        rows = [(it * SSD_UNROLL + u) * cl for u in range(SSD_UNROLL)]
        return rows if isinstance(it, int) else [pl.multiple_of(r, cl) for r in rows]

    def run_strands(strands):
        strands = list(strands)
        while strands:
            for s in list(strands):
                if next(s, "done") == "done":
                    strands.remove(s)

    n_groups = n_chunks // SSD_UNROLL
    run_strands([act_strand(r) for r in group_rows(0)])

    gw = SSD_HPG * SSD_HD
    ri = lax.broadcasted_iota(jnp.int32, (cl, cl), 0)
    ci = lax.broadcasted_iota(jnp.int32, (cl, cl), 1)
    tris = [ri >= ci, ri <= ci]
    tri_bs = [jnp.where(t, 1.0, 0.0).astype(BF16) for t in tris]
    sl = lax.broadcasted_iota(jnp.int32, (LANE, gw), 0)
    sc = lax.broadcasted_iota(jnp.int32, (LANE, gw), 1) // SSD_HD
    sels = [jnp.where(sl == sc + d * SSD_HPG, 1.0, 0.0).astype(BF16) for d in range(2)]
    skip = skip_ref[0, 0:1, :] + skip_ref[0, 1:2, :]

    def split2(v):
        hi = v.astype(BF16)
        return hi, (v - hi.astype(F32)).astype(BF16)

    def local_strand(d, r, shared, out):
        dtc = dtv_ref[pl.ds(r, cl), :]
        a_hi, a_lo = split2(dtc * a_rate)
        cs = _dot(tri_bs[d], a_hi) + _dot(tri_bs[d], a_lo)
        yield
        cs_t = cs.T
        c_hi, c_lo = split2(cs)
        dt_hi, dt_lo = split2(dtc)
        cs_w = _dot(c_hi, sels[d]) + _dot(c_lo, sels[d])
        dt_w = _dot(dt_hi, sels[d]) + _dot(dt_lo, sels[d])
        yield
        cb, bt_b = shared
        tot_w = cs_w[0:1, :] if d else cs_w[cl - 1:cl, :]
        e_ref[d, pl.ds(r, cl), :] = jnp.exp(cs_w)
        xd = act_ref[pl.ds(r, cl), 0:_SSD_B0] * dt_w
        s_ref[d, pl.ds(r, cl), :] = _dot(bt_b, (xd * jnp.exp(tot_w - cs_w)).astype(BF16))
        xd_b = xd.astype(BF16)
        yield
        first_half = lax.broadcasted_iota(jnp.int32, (cl, LANE), 1) < SSD_HD
        tiles = []
        for pair in range(SSD_HPG // 2):
            prods = []
            for j in (2 * pair, 2 * pair + 1):
                ln = d * SSD_HPG + j
                lmat = jnp.exp(jnp.where(tris[d], cs[:, ln:ln + 1] - cs_t[ln:ln + 1, :], -jnp.inf))
                prods.append(_dot((cb * lmat).astype(BF16), xd_b[:, pair * LANE:(pair + 1) * LANE]))
                yield
            tiles.append(jnp.where(first_half, prods[0], prods[1]))
        out.append(jnp.concatenate(tiles, axis=1))

    def local_group(it, prefetch):
        rows = group_rows(it)
        outs = [[] for _ in rows]
        shared = []
        for r in rows:
            bmat = act_ref[pl.ds(r, cl), _SSD_B0:_SSD_C0]
            c_b = act_ref[pl.ds(r, cl), _SSD_C0:_SSD_W].astype(BF16)
            shared.append((_dot_nt(c_b, bmat.astype(BF16)), bmat.T.astype(BF16)))
        strands = [local_strand(d, r, sh, o) for r, sh, o in zip(rows, shared, outs) for d in range(2)]
        if prefetch:
            strands += [act_strand(r) for r in group_rows(it + 1)]
        run_strands(strands)
        for r, o in zip(rows, outs):
            y_ref[0, pl.ds(r, cl), :] = skip * act_ref[pl.ds(r, cl), 0:_SSD_B0] + o[0] + o[1]

    def local_body(it, carry):
        local_group(it, True)
        return carry

    lax.fori_loop(0, n_groups - 1, local_body, 0)
    local_group(n_groups - 1, False)

    h_ref[...] = jnp.zeros(h_ref.shape, F32)

    def scan_body(it, carry):
        hs = [h_ref[0], h_ref[1]]
        todo = []
        for u in range(SSD_UNROLL):
            i = it * SSD_UNROLL + u
            c_fwd = jnp.where(i < n_ctx_chunks, n_lat_chunks + i, i - n_ctx_chunks)
            for d, c in ((0, c_fwd), (1, n_chunks - 1 - i)):
                r = pl.multiple_of(c * cl, cl)
                c_b = act_ref[pl.ds(r, cl), _SSD_C0:_SSD_W].astype(BF16)
                y_off = _dot(c_b, hs[d].astype(BF16)) * e_ref[d, pl.ds(r, cl), :]
                decay = e_ref[d, pl.ds(r + (0 if d else cl - 1), 1), :]
                hs[d] = decay * hs[d] + s_ref[d, pl.ds(r, cl), :]
                todo.append((r, y_off))
        h_ref[0] = hs[0]
        h_ref[1] = hs[1]
        for r, y_off in todo:
            y_ref[0, pl.ds(r, cl), :] = y_ref[0, pl.ds(r, cl), :] + y_off
        return carry

    lax.fori_loop(0, n_chunks // SSD_UNROLL, scan_body, 0)


def _ssd(xbc, dtr, lw, *, n_lat, n_ctx):
    b, t_all, _ = xbc.shape
    gw = SSD_HPG * SSD_HD
    assert (t_all // SSD_CHUNK) % SSD_UNROLL == 0
    n_x = SSD_INNER // SSD_STATE
    xs_blk = lambda i, g: (i, 0, g)
    b_blk = lambda i, g: (i, 0, n_x + g)
    c_blk = lambda i, g: (i, 0, n_x + SSD_GROUPS + g)
    w_cols = lambda rows: [pl.BlockSpec((rows, gw), lambda i, g: (0, g)),
                           pl.BlockSpec((rows, SSD_STATE), lambda i, g: (0, n_x + g)),
                           pl.BlockSpec((rows, SSD_STATE), lambda i, g: (0, n_x + SSD_GROUPS + g))]
    return pl.pallas_call(
        functools.partial(_ssd_kernel, n_lat=n_lat, n_ctx=n_ctx),
        grid=(b, SSD_GROUPS),
        in_specs=[
            pl.BlockSpec((1, t_all, gw), xs_blk),
            pl.BlockSpec((1, t_all, SSD_STATE), b_blk),
            pl.BlockSpec((1, t_all, SSD_STATE), c_blk),
            pl.BlockSpec((1, t_all, LANE), lambda i, g: (i, 0, 0)),
        ] + w_cols(SSD_CONV) + w_cols(1) + [
            pl.BlockSpec((1, 8, LANE), lambda i, g: (g, 0, 0)),
            pl.BlockSpec((1, 2, gw), lambda i, g: (g, 0, 0)),
        ],
        out_specs=pl.BlockSpec((1, t_all, gw), xs_blk),
        out_shape=jax.ShapeDtypeStruct((b, t_all, SSD_INNER), F32),
        scratch_shapes=[
            pltpu.VMEM((t_all + 3 * _PAD_ROWS, _SSD_W), F32),
            pltpu.VMEM((t_all, _SSD_W), F32),
            pltpu.VMEM((t_all, LANE), F32),
            pltpu.VMEM((2, t_all, gw), F32),
            pltpu.VMEM((2, t_all, gw), F32),
            pltpu.VMEM((2, SSD_STATE, gw), F32),
        ],
        compiler_params=_cparams(2),
        name="ssd_scan",
    )(xbc, xbc, xbc, dtr, lw["conv_w"], lw["conv_w"], lw["conv_w"], lw["conv_b"], lw["conv_b"],
      lw["conv_b"], lw["ssd_par"], lw["ssd_skip"])


def _attend_heads(qs, load_ks, load_vts, k_lo, k_hi):
    n = len(qs)
    tasks = [(h, lo, min(lo + KEY_CHUNK, k_hi)) for lo in range(k_lo, k_hi, KEY_CHUNK) for h in range(n)]
    nt = len(tasks)
    m, l, acc = [None] * n, [None] * n, [None] * n
    s, shift, alpha, p = {}, {}, {}, {}
    d_max, d_exp, d_pv = ATTN_STAGES
    for i in range(nt + d_pv):
        if i < nt:
            h, lo, hi = tasks[i]
            s[i] = _dot_nt(load_ks[h](lo, hi), qs[h])
        j = i - d_max
        if 0 <= j < nt:
            h = tasks[j][0]
            cm = jnp.max(s[j], axis=0, keepdims=True)
            if m[h] is None:
                alpha[j] = None
                m[h] = cm
            else:
                m_new = jnp.maximum(m[h], cm)
                alpha[j] = jnp.exp2(m[h] - m_new)
                m[h] = m_new
            shift[j] = m[h]
        j = i - d_exp
        if 0 <= j < nt:
            p[j] = jnp.exp2(s.pop(j) - shift.pop(j)).astype(BF16)
        j = i - d_pv
        if 0 <= j < nt:
            h, lo, hi = tasks[j]
            v_ext = jnp.concatenate([load_vts[h](lo, hi), jnp.ones((SUM_ROWS, hi - lo), BF16)], axis=0)
            pv = _dot(v_ext, p.pop(j))
            a = alpha.pop(j)
            acc[h] = pv if a is None else a * acc[h] + pv
    dv = acc[0].shape[0] - SUM_ROWS
    return [acc[h][0:dv, :] / acc[h][dv:dv + 1, :] for h in range(n)]


def _by_query_tile(compute, n_lat, n_ctx, nq_ctx, with_ctx_queries):
    if not with_ctx_queries:
        compute(0, n_lat + n_ctx)
        return
    qi = pl.program_id(2)

    @pl.when(qi >= nq_ctx)
    def _():
        compute(0, n_lat + n_ctx)

    @pl.when(qi < nq_ctx)
    def _():
        compute(n_lat, n_lat + n_ctx)


def _store_heads(o_ref, outs):
    for i in range(0, len(outs), 2):
        pair = jnp.concatenate(outs[i:i + 2], axis=0).T.astype(BF16)
        o_ref[0, :, i // 2 * LANE:(i // 2 + 1) * LANE] = pair


def _mla_attn_kernel(q_ref, k_ref, vt_ref, o_ref, *, n_lat, n_ctx, nq_ctx, with_ctx_queries):
    def compute(k_lo, k_hi):
        lanes = [slice(i * HEAD_PAD, (i + 1) * HEAD_PAD) for i in range(ATTN_HEADS)]
        rows = [slice(i * MLA_V, (i + 1) * MLA_V) for i in range(ATTN_HEADS)]
        outs = _attend_heads([q_ref[0, :, ln] for ln in lanes],
                             [lambda lo, hi, i=i: k_ref[0, i, lo:hi, :] for i in range(ATTN_HEADS)],
                             [lambda lo, hi, rw=rw: vt_ref[0, rw, lo:hi] for rw in rows], k_lo, k_hi)
        _store_heads(o_ref, outs)

    _by_query_tile(compute, n_lat, n_ctx, nq_ctx, with_ctx_queries)


def _gqa_attn_kernel(q_ref, k_ref, vt_ref, o_ref, *, n_lat, n_ctx, nq_ctx, with_ctx_queries):
    def compute(k_lo, k_hi):
        qs = []
        for i in range(ATTN_HEADS):
            q2 = q_ref[0, :, i // 2 * LANE:(i // 2 + 1) * LANE]
            lane = lax.broadcasted_iota(jnp.int32, q2.shape, 1)
            sel = (lane < GQA_HD) if i % 2 == 0 else (lane >= GQA_HD)
            qs.append(jnp.where(sel, q2, jnp.zeros_like(q2)))
        rep = GQA_HEADS // GQA_KV_HEADS
        v_rows = [slice(i // rep * GQA_HD, (i // rep + 1) * GQA_HD) for i in range(ATTN_HEADS)]
        outs = _attend_heads(qs, [lambda lo, hi, i=i: k_ref[0, i // rep, lo:hi, :] for i in range(ATTN_HEADS)],
                             [lambda lo, hi, rw=rw: vt_ref[0, rw, lo:hi] for rw in v_rows], k_lo, k_hi)
        _store_heads(o_ref, outs)

    _by_query_tile(compute, n_lat, n_ctx, nq_ctx, with_ctx_queries)


def _attention(kind, q, k, v_t, *, n_lat, n_ctx, with_ctx_queries):
    b, t_all, _ = q.shape
    tq = TOKEN_TILE
    nq_lat = n_lat // tq
    nq = t_all // tq if with_ctx_queries else nq_lat
    q_tile = lambda j: (j + nq_lat) % nq
    nh = ATTN_HEADS
    rep = GQA_HEADS // GQA_KV_HEADS
    assert nh % rep == 0 and MLA_HEADS % nh == 0
    if kind == "mla":
        body = _mla_attn_kernel
        in_specs = [
            pl.BlockSpec((1, tq, nh * HEAD_PAD), lambda i, p, j: (i, q_tile(j), p)),
            pl.BlockSpec((1, nh, t_all, HEAD_PAD), lambda i, p, j: (i, p, 0, 0)),
            pl.BlockSpec((1, nh * MLA_V, t_all), lambda i, p, j: (i, p, 0)),
        ]
    else:
        body = _gqa_attn_kernel
        in_specs = [
            pl.BlockSpec((1, tq, nh * GQA_HD), lambda i, p, j: (i, q_tile(j), p)),
            pl.BlockSpec((1, nh // rep, t_all, 2 * GQA_HD), lambda i, p, j: (i, p, 0, 0)),
            pl.BlockSpec((1, nh // rep * GQA_HD, t_all), lambda i, p, j: (i, p, 0)),
        ]
    return pl.pallas_call(
        functools.partial(body, n_lat=n_lat, n_ctx=n_ctx, nq_ctx=nq - nq_lat, with_ctx_queries=with_ctx_queries),
        grid=(b, MLA_HEADS // nh, nq),
        in_specs=in_specs,
        out_specs=pl.BlockSpec((1, tq, nh * MLA_V), lambda i, p, j: (i, q_tile(j), p)),
        out_shape=jax.ShapeDtypeStruct((b, t_all, MLA_HEADS * MLA_V), BF16),
        compiler_params=_cparams(3),
        name=kind + "_attention",
    )(q, k, v_t)


def _out_kernel(x_ref, modx_ref, modc_ref, gpre_ref, gpost_ref, omla_ref, ogqa_ref, y_ref, wzg_ref, gssd_ref,
                wmo_ref, wgo_ref, wso_ref, wout_ref, o_ref, *, tm, n_lat):
    t = pl.program_id(1)
    d = x_ref.shape[2]
    gate_cols = lambda i: slice(SSD_INNER + i * d, SSD_INNER + (i + 1) * d)

    def strand(r0):
        rows = slice(r0, r0 + OUT_SUB)
        mod = lambda k: _mod_rows(t, tm, n_lat, modx_ref, modc_ref, k, r0, OUT_SUB)
        x = x_ref[0, rows, :]
        u = (_rms(x, gpre_ref[...]) * (1.0 + mod(4)) + mod(3)).astype(BF16)
        yield
        z = _dot(u, wzg_ref[:, 0:SSD_INNER])
        a_mla = _dot(omla_ref[0, rows, :], wmo_ref[...])
        g0 = _dot(u, wzg_ref[:, gate_cols(0)])
        yield
        a_gqa = _dot(ogqa_ref[0, rows, :], wgo_ref[...])
        g1 = _dot(u, wzg_ref[:, gate_cols(1)])
        y = _rms(y_ref[0, rows, :] * (z * jax.nn.sigmoid(z)), gssd_ref[...]).astype(BF16)
        yield
        g2 = _dot(u, wzg_ref[:, gate_cols(2)])
        a_ssd = _dot(y, wso_ref[...])
        merged = jax.nn.sigmoid(g0) * a_mla + jax.nn.sigmoid(g1) * a_gqa
        yield
        merged = (merged + jax.nn.sigmoid(g2) * a_ssd).astype(BF16)
        out = _dot(merged, wout_ref[...])
        yield
        o_ref[0, rows, :] = x + mod(5) * _rms(out, gpost_ref[...])

    _emit_skewed([strand(r0) for r0 in range(0, tm, OUT_SUB)])


def _mixer_out(xc, modx, modc, omla, ogqa, y, lw, *, n_lat, n_rows):
    b, t_all, d = xc.shape
    tm = OUT_SUB
    row = lambda w: pl.BlockSpec((1, tm, w), lambda i, j: (i, j, 0))
    res = [lw["wzg"], lw["g_ssd"], lw["w_mla_o"], lw["w_gqa_o"], lw["w_ssd_o"], lw["w_out"]]
    return pl.pallas_call(
        functools.partial(_out_kernel, tm=tm, n_lat=n_lat),
        grid=(b, n_rows // tm),
        in_specs=[row(d), pl.BlockSpec((1, N_MOD, d), lambda i, j: (i, 0, 0)), _resident((N_MOD, d)),
                  _resident((1, d)), _resident((1, d)),
                  row(MLA_HEADS * MLA_V), row(GQA_HEADS * GQA_HD), row(SSD_INNER)]
        + [_resident(w.shape) for w in res],
        out_specs=row(d),
        out_shape=jax.ShapeDtypeStruct((b, t_all, d), F32),
        input_output_aliases={0: 0},
        compiler_params=_cparams(2),
        name="mixer_out",
    )(xc, modx, modc, lw["g_pre1"], lw["g_post1"], omla, ogqa, y, *res)


def _rope_tables(n_lat, n_ctx):
    rows = n_lat // GRID_W
    r = jnp.repeat(jnp.arange(rows, dtype=F32), GRID_W)
    c = jnp.tile(jnp.arange(GRID_W, dtype=F32), rows)

    def angles(rot_dim):
        n_freq = rot_dim // 4
        inv = ROPE_THETA ** (-jnp.arange(n_freq, dtype=F32) / n_freq)
        return jnp.concatenate([r[:, None] * inv, c[:, None] * inv], axis=-1)

    def finish(cos_l, sin_l):
        ident_c = jnp.ones((n_ctx, LANE), F32)
        ident_s = jnp.zeros((n_ctx, LANE), F32)
        return jnp.concatenate([cos_l, ident_c], 0), jnp.concatenate([sin_l, ident_s], 0)

    ang = angles(MLA_ROPE)
    cos, sin = jnp.cos(ang), jnp.sin(ang)
    ones = jnp.ones((n_lat, MLA_NOPE), F32)
    zeros = jnp.zeros((n_lat, MLA_NOPE), F32)
    tail1 = jnp.ones((n_lat, HEAD_PAD - MLA_NOPE - MLA_ROPE), F32)
    tail0 = jnp.zeros((n_lat, HEAD_PAD - MLA_NOPE - MLA_ROPE), F32)
    mla = finish(jnp.concatenate([ones, cos, cos, tail1], 1), jnp.concatenate([zeros, -sin, sin, tail0], 1))

    ang = angles(GQA_HD)
    cos, sin = jnp.cos(ang), jnp.sin(ang)
    gqa = finish(jnp.concatenate([cos, cos, cos, cos], 1), jnp.concatenate([-sin, sin, -sin, sin], 1))
    return mla + gqa


def _head_indicator():
    head = jnp.arange(GQA_HEADS * GQA_HD) // GQA_HD
    e = (head[:, None] == jnp.arange(LANE)[None, :])
    return (e.astype(F32) / GQA_HD).astype(BF16)


def _with_swapped_rope(w_rope):
    half = MLA_ROPE // 2
    return jnp.concatenate([w_rope, w_rope[..., half:], w_rope[..., :half]], axis=-1)


def _group_lanes(v):
    return v.reshape(2, SSD_GROUPS, SSD_HPG).transpose(1, 0, 2).reshape(SSD_GROUPS, 2 * SSD_HPG)


def _layer_weights(l, p):
    d = p["w_in"].shape[1]
    w = p["w_in"][l]
    kg = w[:, _C_KG:_C_VG].reshape(d, GQA_KV_HEADS, GQA_HD)
    kdup = jnp.concatenate([kg, kg], axis=-1).reshape(d, 2 * GQA_KV_HEADS * GQA_HD)
    kr = jnp.concatenate([jnp.zeros((d, MLA_NOPE), F32), _with_swapped_rope(w[:, _C_KROPE:_C_KG])], axis=-1)
    wdt = w[:, _C_DT:_C_QLAT].reshape(d, 2, SSD_GROUPS, SSD_HPG).transpose(0, 2, 1, 3).reshape(d, 2 * SSD_HEADS)
    wdt = jnp.pad(wdt, ((0, 0), (0, LANE - 2 * SSD_HEADS)))
    wa = jnp.concatenate([w[:, _C_KVLAT:_C_KROPE], kdup, w[:, _C_XBC:_C_DT], w[:, _C_QLAT:_C_QG],
                          w[:, _C_QG:_C_Z], kr, wdt], axis=1).astype(BF16)

    wkv = p["w_mla_kv_up"][l].reshape(MLA_KV_RANK, MLA_HEADS, MLA_NOPE + MLA_V)
    wk = jnp.pad(wkv[:, :, :MLA_NOPE], ((0, 0), (0, 0), (0, HEAD_PAD - MLA_NOPE)))
    wk = wk.reshape(MLA_KV_RANK, MLA_HEADS * HEAD_PAD).astype(BF16)
    wv_t = wkv[:, :, MLA_NOPE:].reshape(MLA_KV_RANK, MLA_HEADS * MLA_V).T.astype(BF16)
    wq = p["w_mla_q_up"][l].reshape(MLA_Q_RANK, MLA_HEADS, MLA_NOPE + MLA_ROPE)
    wq = jnp.concatenate([wq[:, :, :MLA_NOPE], _with_swapped_rope(wq[:, :, MLA_NOPE:])], axis=-1)
    wq = wq.reshape(MLA_Q_RANK, MLA_HEADS * HEAD_PAD).astype(BF16)

    par = jnp.zeros((SSD_GROUPS, 8, LANE), F32)
    par = par.at[:, 0, :2 * SSD_HPG].set(_group_lanes(p["dt_bias"][l]))
    par = par.at[:, 1, :2 * SSD_HPG].set(_group_lanes(p["a_log"][l]))
    skip = jnp.repeat(p["ssd_skip"][l].reshape(2, SSD_GROUPS, SSD_HPG), SSD_HD, axis=2).transpose(1, 0, 2)

    row = lambda v: v.reshape(1, -1).astype(F32)
    ffn = lambda s: (p["w_ffn_gate"][l, s].astype(BF16), p["w_ffn_up"][l, s].astype(BF16),
                     p["w_ffn_down"][l, s].astype(BF16))
    return {
        "ffn0": ffn(0), "ffn1": ffn(1),
        "g_pre0": row(p["g_pre"][l, 0]), "g_pre1": row(p["g_pre"][l, 1]), "g_pre2": row(p["g_pre"][l, 2]),
        "g_post0": row(p["g_post"][l, 0]), "g_post1": row(p["g_post"][l, 1]), "g_post2": row(p["g_post"][l, 2]),
        "wa": wa, "wvg_t": w[:, _C_VG:_C_XBC].T.astype(BF16),
        "g_mla_kv": row(p["g_mla_kv"][l]), "wk": wk, "wv_t": wv_t,
        "g_mla_q": row(p["g_mla_q"][l]), "wq": wq, "e": _head_indicator(),
        "g_gqa_k": row(jnp.tile(p["g_gqa_k"][l], 2 * GQA_KV_HEADS)),
        "g_gqa_q": row(jnp.tile(p["g_gqa_q"][l], GQA_HEADS)),
        "conv_w": p["conv_w"][l], "conv_b": row(p["conv_b"][l]), "ssd_par": par, "ssd_skip": skip,
        "wzg": w[:, _C_Z:].astype(BF16), "g_ssd": row(p["g_ssd"][l]),
        "w_mla_o": p["w_mla_o"][l].astype(BF16), "w_gqa_o": p["w_gqa_o"][l].astype(BF16),
        "w_ssd_o": p["w_ssd_o"][l].astype(BF16), "w_out": p["w_out"][l].astype(BF16),
    }


def kernel(x, c, ctx, c_ctx, w_mod, b_mod, g_pre, g_post, w_ffn_gate, w_ffn_up, w_ffn_down, w_in, g_mla_q, w_mla_q_up, g_mla_kv, w_mla_kv_up, g_gqa_q, g_gqa_k, conv_w, conv_b, dt_bias, a_log, ssd_skip, g_ssd, w_mla_o, w_gqa_o, w_ssd_o, w_out):
    b, n_lat, d = x.shape
    n_ctx = ctx.shape[1]
    t_all = n_lat + n_ctx
    depth = w_in.shape[0]
    assert n_lat % TOKEN_TILE == 0 and n_ctx % TOKEN_TILE == 0 and n_lat % GRID_W == 0
    assert w_in.shape[2] == _C_GATE + 3 * d and d == SSD_INNER
    p = dict(g_pre=g_pre, g_post=g_post, w_ffn_gate=w_ffn_gate, w_ffn_up=w_ffn_up, w_ffn_down=w_ffn_down,
             w_in=w_in, g_mla_q=g_mla_q, w_mla_q_up=w_mla_q_up, g_mla_kv=g_mla_kv, w_mla_kv_up=w_mla_kv_up,
             g_gqa_q=g_gqa_q, g_gqa_k=g_gqa_k, conv_w=conv_w, conv_b=conv_b, dt_bias=dt_bias, a_log=a_log,
             ssd_skip=ssd_skip, g_ssd=g_ssd, w_mla_o=w_mla_o, w_gqa_o=w_gqa_o, w_ssd_o=w_ssd_o, w_out=w_out)

    rows = -(-(b + 1) // 8) * 8
    c_all = jnp.concatenate([c, c_ctx[None, :], jnp.zeros((rows - b - 1, d), F32)], axis=0)
    mod = _modulation(c_all, w_mod, b_mod).reshape(depth, rows, N_MOD, d)
    tabs = _rope_tables(n_lat, n_ctx)
    xc = jnp.concatenate([x, ctx], axis=1)

    for l in range(depth):
        last = l == depth - 1
        lw = _layer_weights(l, p)
        modx, modc = mod[l, :b], mod[l, b]
        xc = _ffn(xc, modx, modc, lw["g_pre0"], lw["g_post0"], *lw["ffn0"],
                  k0=0, n_lat=n_lat, n_rows=t_all, alias=True)
        kmla, vtmla, qmla, kgqa, vtgqa, qgqa, xbc, dtr = _in_proj(xc, modx, modc, lw, tabs, n_lat=n_lat)
        y = _ssd(xbc, dtr, lw, n_lat=n_lat, n_ctx=n_ctx)
        omla = _attention("mla", qmla, kmla, vtmla, n_lat=n_lat, n_ctx=n_ctx, with_ctx_queries=not last)
        ogqa = _attention("gqa", qgqa, kgqa, vtgqa, n_lat=n_lat, n_ctx=n_ctx, with_ctx_queries=not last)
        n_rows = n_lat if last else t_all
        xc = _mixer_out(xc, modx, modc, omla, ogqa, y, lw, n_lat=n_lat, n_rows=n_rows)
        xc = _ffn(xc, modx, modc, lw["g_pre2"], lw["g_post2"], *lw["ffn1"],
                  k0=6, n_lat=n_lat, n_rows=n_rows, alias=not last)
    return xc
```

```python
import functools
import math

import jax
import jax.numpy as jnp
from jax import lax
from jax.experimental import pallas as pl
from jax.experimental.pallas import tpu as pltpu

F32 = jnp.float32
BF16 = jnp.bfloat16

GRID_W = 64
ROPE_THETA = 10000.0
NORM_EPS = 1e-6
FFN_RES = 0.5
N_MOD = 9

MLA_HEADS = 16
MLA_NOPE = 64
MLA_ROPE = 32
MLA_V = 64
MLA_Q_RANK = 512
MLA_KV_RANK = 256
GQA_HEADS = 16
GQA_KV_HEADS = 4
GQA_HD = 64
SSD_INNER = 1024
SSD_HD = 64
SSD_HEADS = 16
SSD_GROUPS = 4
SSD_HPG = 4
SSD_STATE = 128
SSD_CONV = 5
SSD_CHUNK = 128
SSD_CONV_DIM = SSD_INNER + 2 * SSD_GROUPS * SSD_STATE
MLA_SCALE = (MLA_NOPE + MLA_ROPE) ** -0.5
GQA_SCALE = GQA_HD ** -0.5
LOG2E = math.log2(math.e)

LANE = 128
HEAD_PAD = 128
TOKEN_TILE = 256
SUB_TILE = 256
OUT_SUB = 256
FFN_COLS = 768
KEY_CHUNK = 256
ATTN_HEADS = 16
SUM_ROWS = 16
ATTN_STAGES = (4, 8, 8)
VMEM_LIMIT = 56 * 1024 * 1024

_C_KVLAT = 0
_C_KROPE = _C_KVLAT + MLA_KV_RANK
_C_KG = _C_KROPE + MLA_ROPE
_C_VG = _C_KG + GQA_KV_HEADS * GQA_HD
_C_XBC = _C_VG + GQA_KV_HEADS * GQA_HD
_C_DT = _C_XBC + SSD_CONV_DIM
_C_QLAT = _C_DT + 2 * SSD_HEADS
_C_QG = _C_QLAT + MLA_Q_RANK
_C_Z = _C_QG + GQA_HEADS * GQA_HD
_C_GATE = _C_Z + SSD_INNER

_A_KVLAT = 0
_A_KDUP = _A_KVLAT + MLA_KV_RANK
_A_XBC = _A_KDUP + 2 * GQA_KV_HEADS * GQA_HD
_A_QLAT = _A_XBC + SSD_CONV_DIM
_A_QG = _A_QLAT + MLA_Q_RANK
_A_KR = _A_QG + GQA_HEADS * GQA_HD
_A_DT = _A_KR + HEAD_PAD
_A_END = _A_DT + LANE


def _cparams(n_axes):
    return pltpu.CompilerParams(
        dimension_semantics=("parallel",) * n_axes, vmem_limit_bytes=VMEM_LIMIT)


def _resident(shape):
    nd = len(shape)
    return pl.BlockSpec(shape, lambda *_: (0,) * nd, pipeline_mode=pl.Buffered(1))


def _rms(t, g):
    return t * lax.rsqrt(jnp.mean(t * t, axis=-1, keepdims=True) + NORM_EPS) * g


def _dot(a, b):
    return jnp.dot(a, b, preferred_element_type=F32)


def _dot_nt(a, b):
    return lax.dot_general(a, b, (((1,), (1,)), ((), ())), preferred_element_type=F32)


def _mod_rows(tile_idx, tm, n_lat, modx_ref, modc_ref, k, row0=0, n_rows=None):
    n_rows = tm if n_rows is None else n_rows
    rows = tile_idx * tm + row0 + lax.broadcasted_iota(jnp.int32, (n_rows, 1), 0)
    return jnp.where(rows >= n_lat, modc_ref[k:k + 1, :], modx_ref[0, k:k + 1, :])


def _emit_skewed(strands):
    pending, active = list(strands), []
    while pending or active:
        if pending:
            active.append(pending.pop(0))
        for s in list(active):
            if next(s, "done") == "done":
                active.remove(s)


def _mod_kernel(c_ref, w_ref, b_ref, o_ref):
    s = c_ref[...]
    s = s * jax.nn.sigmoid(s)
    o_ref[0] = jnp.dot(s, w_ref[0], preferred_element_type=F32,
                       precision=lax.Precision.HIGHEST) + b_ref[0]


def _modulation(c_all, w_mod, b_mod):
    n_layers, d, _ = w_mod.shape
    rows = c_all.shape[0]
    return pl.pallas_call(
        _mod_kernel,
        grid=(n_layers, N_MOD),
        in_specs=[
            pl.BlockSpec((rows, d), lambda l, j: (0, 0)),
            pl.BlockSpec((1, d, d), lambda l, j: (l, 0, j)),
            pl.BlockSpec((1, 1, d), lambda l, j: (l, 0, j)),
        ],
        out_specs=pl.BlockSpec((1, rows, d), lambda l, j: (l, 0, j)),
        out_shape=jax.ShapeDtypeStruct((n_layers, rows, N_MOD * d), F32),
        compiler_params=_cparams(2),
        name="modulation",
    )(c_all, w_mod, b_mod.reshape(n_layers, 1, N_MOD * d))


def _ffn_kernel(x_ref, modx_ref, modc_ref, gin_ref, gout_ref, wg_ref, wu_ref, wd_ref, o_ref,
                *, tm, n_lat, k0):
    t = pl.program_id(1)
    f = wg_ref.shape[1]
    cuts = list(range(0, f, FFN_COLS)) + [f]

    def strand(r0):
        rows = slice(r0, r0 + SUB_TILE)
        mod = lambda k: _mod_rows(t, tm, n_lat, modx_ref, modc_ref, k, r0, SUB_TILE)
        x = x_ref[0, rows, :]
        u = (_rms(x, gin_ref[...]) * (1.0 + mod(k0 + 1)) + mod(k0)).astype(BF16)
        yield
        hidden = []
        for lo, hi in zip(cuts[:-1], cuts[1:]):
            hidden.append((_dot(u, wg_ref[:, lo:hi]), _dot(u, wu_ref[:, lo:hi])))
            yield
        y = None
        for (hg, hu), lo, hi in zip(hidden, cuts[:-1], cuts[1:]):
            a = (hg * jax.nn.sigmoid(hg) * hu).astype(BF16)
            part = _dot(a, wd_ref[lo:hi, :])
            y = part if y is None else y + part
            yield
        o_ref[0, rows, :] = x + (FFN_RES * mod(k0 + 2)) * _rms(y, gout_ref[...])

    _emit_skewed([strand(r0) for r0 in range(0, tm, SUB_TILE)])


def _row_tile(n_rows):
    for tm in (4 * SUB_TILE, 3 * SUB_TILE, 2 * SUB_TILE):
        if n_rows % tm == 0:
            return tm
    return SUB_TILE


def _ffn(xc, modx, modc, g_in, g_out, wg, wu, wd, *, k0, n_lat, n_rows, alias):
    b, t_all, d = xc.shape
    f = wg.shape[1]
    tm = _row_tile(n_rows)
    out_rows = t_all if alias else n_rows
    return pl.pallas_call(
        functools.partial(_ffn_kernel, tm=tm, n_lat=n_lat, k0=k0),
        grid=(b, n_rows // tm),
        in_specs=[
            pl.BlockSpec((1, tm, d), lambda i, j: (i, j, 0)),
            pl.BlockSpec((1, N_MOD, d), lambda i, j: (i, 0, 0)),
            _resident((N_MOD, d)),
            _resident((1, d)),
            _resident((1, d)),
            _resident((d, f)),
            _resident((d, f)),
            _resident((f, d)),
        ],
        out_specs=pl.BlockSpec((1, tm, d), lambda i, j: (i, j, 0)),
        out_shape=jax.ShapeDtypeStruct((b, out_rows, d), F32),
        input_output_aliases={0: 0} if alias else {},
        compiler_params=_cparams(2),
        name="ffn_half_step",
    )(xc, modx, modc, g_in, g_out, wg, wu, wd)


def _tile_lanes(t, reps):
    return t if reps == 1 else jnp.concatenate([t] * reps, axis=1)


def _rope_gqa(x, cos_t, sin_t):
    w = x.shape[1]
    reps = w // LANE
    lane = lax.broadcasted_iota(jnp.int32, x.shape, 1) % GQA_HD
    half = GQA_HD // 2
    partner = jnp.where(lane < half, pltpu.roll(x, w - half, 1), pltpu.roll(x, half, 1))
    return x * _tile_lanes(cos_t, reps) + partner * _tile_lanes(sin_t, reps)


def _rope_mla(x, cos_t, sin_t):
    w = x.shape[1]
    partner = pltpu.roll(x, w - MLA_ROPE, 1)
    return x * _tile_lanes(cos_t, w // LANE) + partner * _tile_lanes(sin_t, w // LANE)


def _head_scale(x, ms, gain):
    r = lax.rsqrt(ms + NORM_EPS)
    lane = lax.broadcasted_iota(jnp.int32, r.shape, 1)
    second = (lane >= GQA_HD).astype(jnp.int32)
    tiles = [jnp.take_along_axis(r, 2 * j + second, axis=1) for j in range(x.shape[1] // LANE)]
    return x * jnp.concatenate(tiles, axis=1) * gain


def _in_proj_kernel(x_ref, modx_ref, modc_ref, gpre_ref, wa_ref, wvg_t_ref, gkv_ref, wk_ref, wv_t_ref,
                    gq_ref, wq_ref, e_ref, ggk_ref, ggq_ref,
                    mla_cos_ref, mla_sin_ref, gqa_cos_ref, gqa_sin_ref,
                    kmla_ref, vtmla_ref, qmla_ref, kgqa_ref, vtgqa_ref, qgqa_ref, xbc_ref, dt_ref,
                    *, tm, n_lat):
    t = pl.program_id(1)
    x = x_ref[0]
    shift = _mod_rows(t, tm, n_lat, modx_ref, modc_ref, 3)
    scale = _mod_rows(t, tm, n_lat, modx_ref, modc_ref, 4)
    u = (_rms(x, gpre_ref[...]) * (1.0 + scale) + shift).astype(BF16)
    proj = lambda lo, hi: _dot(u, wa_ref[:, lo:hi])
    mla_cos = mla_cos_ref[...]
    mla_sin = mla_sin_ref[...]
    gqa_cos = gqa_cos_ref[...]
    gqa_sin = gqa_sin_ref[...]
    n_k = 2 * GQA_KV_HEADS * GQA_HD

    h_q = proj(_A_QLAT, _A_QG)
    h_qg = proj(_A_QG, _A_KR)
    cq = _rms(h_q, gq_ref[...]).astype(BF16)
    q_mla = _dot(cq, wq_ref[...])
    ms_q = _dot((h_qg * h_qg).astype(BF16), e_ref[...])
    h_kv = proj(_A_KVLAT, _A_KDUP)
    h_kd = proj(_A_KDUP, _A_XBC)
    h_kr = proj(_A_KR, _A_END)
    qmla_ref[0] = (_rope_mla(q_mla, mla_cos, mla_sin) * (MLA_SCALE * LOG2E)).astype(BF16)
    ckv = _rms(h_kv, gkv_ref[...]).astype(BF16)
    k_nope = _dot(ckv, wk_ref[...])
    v_t = _dot_nt(wv_t_ref[...], ckv)
    ms_k = _dot((h_kd * h_kd).astype(BF16), e_ref[0:n_k, :])
    vg_t = _dot_nt(wvg_t_ref[...], u)
    qg = _head_scale(h_qg, ms_q, ggq_ref[...])
    qgqa_ref[0] = (_rope_gqa(qg, gqa_cos, gqa_sin) * (GQA_SCALE * LOG2E)).astype(BF16)

    dt_ref[0] = h_kr[:, HEAD_PAD:]
    k_rope = _rope_mla(h_kr[:, 0:HEAD_PAD], mla_cos, mla_sin)
    k_rope = jnp.where(lax.broadcasted_iota(jnp.int32, k_rope.shape, 1) < MLA_NOPE + MLA_ROPE, k_rope, 0.0)
    k_mla =(k_nope + _tile_lanes(k_rope, MLA_HEADS)).astype(BF16)
    for hd in range(MLA_HEADS):
        kmla_ref[0, hd] = k_mla[:, hd * HEAD_PAD:(hd + 1) * HEAD_PAD]
    vtmla_ref[0] = v_t.astype(BF16)
    kd = _head_scale(h_kd, ms_k, ggk_ref[...])
    k_gqa = _rope_gqa(kd, gqa_cos, gqa_sin).astype(BF16)
    for hd in range(GQA_KV_HEADS):
        kgqa_ref[0, hd] = k_gqa[:, hd * LANE:(hd + 1) * LANE]
    vtgqa_ref[0] = vg_t.astype(BF16)
    xbc_ref[0] = proj(_A_XBC, _A_QLAT)


def _in_proj(xc, modx, modc, lw, tabs, *, n_lat):
    b, t_all, d = xc.shape
    tm = TOKEN_TILE
    n_qm = MLA_HEADS * HEAD_PAD
    n_qg = GQA_HEADS * GQA_HD
    n_vm = MLA_HEADS * MLA_V
    n_vg = GQA_KV_HEADS * GQA_HD
    row = lambda w: pl.BlockSpec((1, tm, w), lambda i, j: (i, j, 0))
    col = lambda h: pl.BlockSpec((1, h, tm), lambda i, j: (i, 0, j))
    tab = pl.BlockSpec((tm, LANE), lambda i, j: (j, 0))
    res = [lw["g_pre1"], lw["wa"], lw["wvg_t"], lw["g_mla_kv"], lw["wk"], lw["wv_t"],
           lw["g_mla_q"], lw["wq"], lw["e"], lw["g_gqa_k"], lw["g_gqa_q"]]
    return pl.pallas_call(
        functools.partial(_in_proj_kernel, tm=tm, n_lat=n_lat),
        grid=(b, t_all // tm),
        in_specs=[row(d), pl.BlockSpec((1, N_MOD, d), lambda i, j: (i, 0, 0)), _resident((N_MOD, d))]
        + [_resident(w.shape) for w in res] + [tab] * 4,
        out_specs=[pl.BlockSpec((1, MLA_HEADS, tm, HEAD_PAD), lambda i, j: (i, 0, j, 0)),
                   col(n_vm), row(n_qm), pl.BlockSpec((1, GQA_KV_HEADS, tm, LANE), lambda i, j: (i, 0, j, 0)),
                   col(n_vg), row(n_qg), row(SSD_CONV_DIM), row(LANE)],
        out_shape=[
            jax.ShapeDtypeStruct((b, MLA_HEADS, t_all, HEAD_PAD), BF16),
            jax.ShapeDtypeStruct((b, n_vm, t_all), BF16),
            jax.ShapeDtypeStruct((b, t_all, n_qm), BF16),
            jax.ShapeDtypeStruct((b, GQA_KV_HEADS, t_all, LANE), BF16),
            jax.ShapeDtypeStruct((b, n_vg, t_all), BF16),
            jax.ShapeDtypeStruct((b, t_all, n_qg), BF16),
            jax.ShapeDtypeStruct((b, t_all, SSD_CONV_DIM), F32),
            jax.ShapeDtypeStruct((b, t_all, LANE), F32),
        ],
        compiler_params=_cparams(2),
        name="in_proj",
    )(xc, modx, modc, *res, *tabs)


_SSD_W = SSD_HPG * SSD_HD + 2 * SSD_STATE
_SSD_B0 = SSD_HPG * SSD_HD
_SSD_C0 = _SSD_B0 + SSD_STATE
_PAD_ROWS = 8
SSD_UNROLL = 3
_ACT_COLS = 256


def _softplus(v):
    return jnp.maximum(v, 0.0) + jnp.log1p(jnp.exp(-jnp.abs(v)))


def _ssd_kernel(xs_ref, bm_ref, cm_ref, dt_ref, wx_ref, wb_ref, wc_ref, bx_ref, bb_ref, bc_ref,
                par_ref, skip_ref, y_ref, pad_ref, act_ref, dtv_ref, e_ref, s_ref, h_ref, *, n_lat, n_ctx):
    cl = SSD_CHUNK
    t_all = n_lat + n_ctx
    n_chunks = t_all // cl
    n_ctx_chunks = n_ctx // cl
    n_lat_chunks = n_lat // cl
    g = pl.program_id(1)

    def pad_row(r):
        return pl.multiple_of(r + jnp.where(r >= n_lat, 2 * _PAD_ROWS, _PAD_ROWS), _PAD_ROWS)

    zeros = jnp.zeros((_PAD_ROWS, _SSD_W), F32)
    pad_ref[0:_PAD_ROWS, :] = zeros
    pad_ref[_PAD_ROWS + n_lat:2 * _PAD_ROWS + n_lat, :] = zeros
    pad_ref[2 * _PAD_ROWS + t_all:3 * _PAD_ROWS + t_all, :] = zeros

    def copy_body(i, carry):
        r = pl.multiple_of(i * cl, cl)
        dst = pad_row(r)
        pad_ref[pl.ds(dst, cl), 0:_SSD_B0] = xs_ref[0, pl.ds(r, cl), :]
        pad_ref[pl.ds(dst, cl), _SSD_B0:_SSD_C0] = bm_ref[0, pl.ds(r, cl), :]
        pad_ref[pl.ds(dst, cl), _SSD_C0:_SSD_W] = cm_ref[0, pl.ds(r, cl), :]
        return carry

    lax.fori_loop(0, n_chunks, copy_body, 0)

    conv_w = jnp.concatenate([wx_ref[...], wb_ref[...], wc_ref[...]], axis=1)
    conv_b = jnp.concatenate([bx_ref[...], bb_ref[...], bc_ref[...]], axis=1)
    par = par_ref[0]
    dt_bias = par[0:1, :]
    a_rate = -jnp.exp(par[1:2, :])
    lane_shift = lax.rem(LANE - 2 * SSD_HPG * g, LANE)
    half_win = SSD_CONV // 2

    def act_strand(r):
        base = pad_row(r)
        for c0 in range(0, _SSD_W, _ACT_COLS):
            cols = slice(c0, c0 + _ACT_COLS)
            blk = pad_ref[pl.ds(base - _PAD_ROWS, cl + 2 * _PAD_ROWS), cols]
            acc = conv_b[:, cols]
            for k in range(SSD_CONV):
                off = _PAD_ROWS + k - half_win
                acc = acc + conv_w[k:k + 1, cols] * blk[off:off + cl, :]
            act_ref[pl.ds(r, cl), cols] = acc * jax.nn.sigmoid(acc)
            yield
        dtv_ref[pl.ds(r, cl), :] = _softplus(pltpu.roll(dt_ref[0, pl.ds(r, cl), :], lane_shift, 1) + dt_bias)

    def group_rows(it):
        rows = [(it * SSD_UNROLL + u) * cl for u in range(SSD_UNROLL)]
        return rows if isinstance(it, int) else [pl.multiple_of(r, cl) for r in rows]

    def run_strands(strands):
        strands = list(strands)
        while strands:
            for s in list(strands):
                if next(s, "done") == "done":
                    strands.remove(s)

    n_groups = n_chunks // SSD_UNROLL
    run_strands([act_strand(r) for r in group_rows(0)])

    gw = SSD_HPG * SSD_HD
    ri = lax.broadcasted_iota(jnp.int32, (cl, cl), 0)
    ci = lax.broadcasted_iota(jnp.int32, (cl, cl), 1)
    tris = [ri >= ci, ri <= ci]
    tri_bs = [jnp.where(t, 1.0, 0.0).astype(BF16) for t in tris]
    sl = lax.broadcasted_iota(jnp.int32, (LANE, gw), 0)
    sc = lax.broadcasted_iota(jnp.int32, (LANE, gw), 1) // SSD_HD
    sels = [jnp.where(sl == sc + d * SSD_HPG, 1.0, 0.0).astype(BF16) for d in range(2)]
    skip = skip_ref[0, 0:1, :] + skip_ref[0, 1:2, :]

    def split2(v):
        hi = v.astype(BF16)
        return hi, (v - hi.astype(F32)).astype(BF16)

    def local_strand(d, r, shared, out):
        dtc = dtv_ref[pl.ds(r, cl), :]
        a_hi, a_lo = split2(dtc * a_rate)
        cs = _dot(tri_bs[d], a_hi) + _dot(tri_bs[d], a_lo)
        yield
        cs_t = cs.T
        c_hi, c_lo = split2(cs)
        dt_hi, dt_lo = split2(dtc)
        cs_w = _dot(c_hi, sels[d]) + _dot(c_lo, sels[d])
        dt_w = _dot(dt_hi, sels[d]) + _dot(dt_lo, sels[d])
        yield
        cb, bt_b = shared
        tot_w = cs_w[0:1, :] if d else cs_w[cl - 1:cl, :]
        e_ref[d, pl.ds(r, cl), :] = jnp.exp(cs_w)
        xd = act_ref[pl.ds(r, cl), 0:_SSD_B0] * dt_w
        s_ref[d, pl.ds(r, cl), :] = _dot(bt_b, (xd * jnp.exp(tot_w - cs_w)).astype(BF16))
        xd_b = xd.astype(BF16)
        yield
        first_half = lax.broadcasted_iota(jnp.int32, (cl, LANE), 1) < SSD_HD
        tiles = []
        for pair in range(SSD_HPG // 2):
            prods = []
            for j in (2 * pair, 2 * pair + 1):
                ln = d * SSD_HPG + j
                lmat = jnp.exp(jnp.where(tris[d], cs[:, ln:ln + 1] - cs_t[ln:ln + 1, :], -jnp.inf))
                prods.append(_dot((cb * lmat).astype(BF16), xd_b[:, pair * LANE:(pair + 1) * LANE]))
                yield
            tiles.append(jnp.where(first_half, prods[0], prods[1]))
        out.append(jnp.concatenate(tiles, axis=1))

    def local_group(it, prefetch):
        rows = group_rows(it)
        outs = [[] for _ in rows]
        shared = []
        for r in rows:
            bmat = act_ref[pl.ds(r, cl), _SSD_B0:_SSD_C0]
            c_b = act_ref[pl.ds(r, cl), _SSD_C0:_SSD_W].astype(BF16)
            shared.append((_dot_nt(c_b, bmat.astype(BF16)), bmat.T.astype(BF16)))
        strands = [local_strand(d, r, sh, o) for r, sh, o in zip(rows, shared, outs) for d in range(2)]
        if prefetch:
            strands += [act_strand(r) for r in group_rows(it + 1)]
        run_strands(strands)
        for r, o in zip(rows, outs):
            y_ref[0, pl.ds(r, cl), :] = skip * act_ref[pl.ds(r, cl), 0:_SSD_B0] + o[0] + o[1]

    def local_body(it, carry):
        local_group(it, True)
        return carry

    lax.fori_loop(0, n_groups - 1, local_body, 0)
    local_group(n_groups - 1, False)

    h_ref[...] = jnp.zeros(h_ref.shape, F32)

    def scan_body(it, carry):
        hs = [h_ref[0], h_ref[1]]
        todo = []
        for u in range(SSD_UNROLL):
            i = it * SSD_UNROLL + u
            c_fwd = jnp.where(i < n_ctx_chunks, n_lat_chunks + i, i - n_ctx_chunks)
            for d, c in ((0, c_fwd), (1, n_chunks - 1 - i)):
                r = pl.multiple_of(c * cl, cl)
                c_b = act_ref[pl.ds(r, cl), _SSD_C0:_SSD_W].astype(BF16)
                y_off = _dot(c_b, hs[d].astype(BF16)) * e_ref[d, pl.ds(r, cl), :]
                decay = e_ref[d, pl.ds(r + (0 if d else cl - 1), 1), :]
                hs[d] = decay * hs[d] + s_ref[d, pl.ds(r, cl), :]
                todo.append((r, y_off))
        h_ref[0] = hs[0]
        h_ref[1] = hs[1]
        for r, y_off in todo:
            y_ref[0, pl.ds(r, cl), :] = y_ref[0, pl.ds(r, cl), :] + y_off
        return carry

    lax.fori_loop(0, n_chunks // SSD_UNROLL, scan_body, 0)


def _ssd(xbc, dtr, lw, *, n_lat, n_ctx):
    b, t_all, _ = xbc.shape
    gw = SSD_HPG * SSD_HD
    assert (t_all // SSD_CHUNK) % SSD_UNROLL == 0
    n_x = SSD_INNER // SSD_STATE
    xs_blk = lambda i, g: (i, 0, g)
    b_blk = lambda i, g: (i, 0, n_x + g)
    c_blk = lambda i, g: (i, 0, n_x + SSD_GROUPS + g)
    w_cols = lambda rows: [pl.BlockSpec((rows, gw), lambda i, g: (0, g)),
                           pl.BlockSpec((rows, SSD_STATE), lambda i, g: (0, n_x + g)),
                           pl.BlockSpec((rows, SSD_STATE), lambda i, g: (0, n_x + SSD_GROUPS + g))]
    return pl.pallas_call(
        functools.partial(_ssd_kernel, n_lat=n_lat, n_ctx=n_ctx),
        grid=(b, SSD_GROUPS),
        in_specs=[
            pl.BlockSpec((1, t_all, gw), xs_blk),
            pl.BlockSpec((1, t_all, SSD_STATE), b_blk),
            pl.BlockSpec((1, t_all, SSD_STATE), c_blk),
            pl.BlockSpec((1, t_all, LANE), lambda i, g: (i, 0, 0)),
        ] + w_cols(SSD_CONV) + w_cols(1) + [
            pl.BlockSpec((1, 8, LANE), lambda i, g: (g, 0, 0)),
            pl.BlockSpec((1, 2, gw), lambda i, g: (g, 0, 0)),
        ],
        out_specs=pl.BlockSpec((1, t_all, gw), xs_blk),
        out_shape=jax.ShapeDtypeStruct((b, t_all, SSD_INNER), F32),
        scratch_shapes=[
            pltpu.VMEM((t_all + 3 * _PAD_ROWS, _SSD_W), F32),
            pltpu.VMEM((t_all, _SSD_W), F32),
            pltpu.VMEM((t_all, LANE), F32),
            pltpu.VMEM((2, t_all, gw), F32),
            pltpu.VMEM((2, t_all, gw), F32),
            pltpu.VMEM((2, SSD_STATE, gw), F32),
        ],
        compiler_params=_cparams(2),
        name="ssd_scan",
    )(xbc, xbc, xbc, dtr, lw["conv_w"], lw["conv_w"], lw["conv_w"], lw["conv_b"], lw["conv_b"],
      lw["conv_b"], lw["ssd_par"], lw["ssd_skip"])


def _attend_heads(qs, load_ks, load_vts, k_lo, k_hi):
    n = len(qs)
    tasks = [(h, lo, min(lo + KEY_CHUNK, k_hi)) for lo in range(k_lo, k_hi, KEY_CHUNK) for h in range(n)]
    nt = len(tasks)
    m, acc = [None] * n, [None] * n
    s, shift, alpha, p = {}, {}, {}, {}
    d_max, d_exp, d_pv = ATTN_STAGES
    for i in range(nt + d_pv):
        if i < nt:
            h, lo, hi = tasks[i]
            s[i] = _dot_nt(load_ks[h](lo, hi), qs[h])
        j = i - d_max
        if 0 <= j < nt:
            h = tasks[j][0]
            cm = jnp.max(s[j], axis=0, keepdims=True)
            if m[h] is None:
                alpha[j] = None
                m[h] = cm
            else:
                m_new = jnp.maximum(m[h], cm)
                alpha[j] = jnp.exp2(m[h] - m_new)
                m[h] = m_new
            shift[j] = m[h]
        j = i - d_exp
        if 0 <= j < nt:
            p[j] = jnp.exp2(s.pop(j) - shift.pop(j)).astype(BF16)
        j = i - d_pv
        if 0 <= j < nt:
            h, lo, hi = tasks[j]
            v_ext = jnp.concatenate([load_vts[h](lo, hi), jnp.ones((SUM_ROWS, hi - lo), BF16)], axis=0)
            pv = _dot(v_ext, p.pop(j))
            a = alpha.pop(j)
            acc[h] = pv if a is None else a * acc[h] + pv
    dv = acc[0].shape[0] - SUM_ROWS
    return [acc[h][0:dv, :] / acc[h][dv:dv + 1, :] for h in range(n)]


def _by_query_tile(compute, n_lat, n_ctx, nq_ctx, with_ctx_queries):
    if not with_ctx_queries:
        compute(0, n_lat + n_ctx)
        return
    qi = pl.program_id(2)

    @pl.when(qi >= nq_ctx)
    def _():
        compute(0, n_lat + n_ctx)

    @pl.when(qi < nq_ctx)
    def _():
        compute(n_lat, n_lat + n_ctx)


def _store_heads(o_ref, outs):
    for i in range(0, len(outs), 2):
        pair = jnp.concatenate(outs[i:i + 2], axis=0).T.astype(BF16)
        o_ref[0, :, i // 2 * LANE:(i // 2 + 1) * LANE] = pair


def _mla_attn_kernel(q_ref, k_ref, vt_ref, o_ref, *, n_lat, n_ctx, nq_ctx, with_ctx_queries):
    def compute(k_lo, k_hi):
        lanes = [slice(i * HEAD_PAD, (i + 1) * HEAD_PAD) for i in range(ATTN_HEADS)]
        rows = [slice(i * MLA_V, (i + 1) * MLA_V) for i in range(ATTN_HEADS)]
        outs = _attend_heads([q_ref[0, :, ln] for ln in lanes],
                             [lambda lo, hi, i=i: k_ref[0, i, lo:hi, :] for i in range(ATTN_HEADS)],
                             [lambda lo, hi, rw=rw: vt_ref[0, rw, lo:hi] for rw in rows], k_lo, k_hi)
        _store_heads(o_ref, outs)

    _by_query_tile(compute, n_lat, n_ctx, nq_ctx, with_ctx_queries)


def _gqa_attn_kernel(q_ref, k_ref, vt_ref, o_ref, *, n_lat, n_ctx, nq_ctx, with_ctx_queries):
    def compute(k_lo, k_hi):
        qs = []
        for i in range(ATTN_HEADS):
            q2 = q_ref[0, :, i // 2 * LANE:(i // 2 + 1) * LANE]
            lane = lax.broadcasted_iota(jnp.int32, q2.shape, 1)
            sel = (lane < GQA_HD) if i % 2 == 0 else (lane >= GQA_HD)
            qs.append(jnp.where(sel, q2, jnp.zeros_like(q2)))
        rep = GQA_HEADS // GQA_KV_HEADS
        v_rows = [slice(i // rep * GQA_HD, (i // rep + 1) * GQA_HD) for i in range(ATTN_HEADS)]
        outs = _attend_heads(qs, [lambda lo, hi, i=i: k_ref[0, i // rep, lo:hi, :] for i in range(ATTN_HEADS)],
                             [lambda lo, hi, rw=rw: vt_ref[0, rw, lo:hi] for rw in v_rows], k_lo, k_hi)
        _store_heads(o_ref, outs)

    _by_query_tile(compute, n_lat, n_ctx, nq_ctx, with_ctx_queries)


def _attention(kind, q, k, v_t, *, n_lat, n_ctx, with_ctx_queries):
    b, t_all, _ = q.shape
    tq = TOKEN_TILE
    nq_lat = n_lat // tq
    nq = t_all // tq if with_ctx_queries else nq_lat
    q_tile = lambda j: (j + nq_lat) % nq
    nh = ATTN_HEADS
    rep = GQA_HEADS // GQA_KV_HEADS
    assert nh % rep == 0 and MLA_HEADS % nh == 0
    if kind == "mla":
        body = _mla_attn_kernel
        in_specs = [
            pl.BlockSpec((1, tq, nh * HEAD_PAD), lambda i, p, j: (i, q_tile(j), p)),
            pl.BlockSpec((1, nh, t_all, HEAD_PAD), lambda i, p, j: (i, p, 0, 0)),
            pl.BlockSpec((1, nh * MLA_V, t_all), lambda i, p, j: (i, p, 0)),
        ]
    else:
        body = _gqa_attn_kernel
        in_specs = [
            pl.BlockSpec((1, tq, nh * GQA_HD), lambda i, p, j: (i, q_tile(j), p)),
            pl.BlockSpec((1, nh // rep, t_all, 2 * GQA_HD), lambda i, p, j: (i, p, 0, 0)),
            pl.BlockSpec((1, nh // rep * GQA_HD, t_all), lambda i, p, j: (i, p, 0)),
        ]
    return pl.pallas_call(
        functools.partial(body, n_lat=n_lat, n_ctx=n_ctx, nq_ctx=nq - nq_lat, with_ctx_queries=with_ctx_queries),
        grid=(b, MLA_HEADS // nh, nq),
        in_specs=in_specs,
        out_specs=pl.BlockSpec((1, tq, nh * MLA_V), lambda i, p, j: (i, q_tile(j), p)),
        out_shape=jax.ShapeDtypeStruct((b, t_all, MLA_HEADS * MLA_V), BF16),
        compiler_params=_cparams(3),
        name=kind + "_attention",
    )(q, k, v_t)


def _out_kernel(x_ref, modx_ref, modc_ref, gpre_ref, gpost_ref, omla_ref, ogqa_ref, y_ref, wzg_ref, gssd_ref,
                wmo_ref, wgo_ref, wso_ref, wout_ref, o_ref, *, tm, n_lat):
    t = pl.program_id(1)
    d = x_ref.shape[2]
    gate_cols = lambda i: slice(SSD_INNER + i * d, SSD_INNER + (i + 1) * d)

    def strand(r0):
        rows = slice(r0, r0 + OUT_SUB)
        mod = lambda k: _mod_rows(t, tm, n_lat, modx_ref, modc_ref, k, r0, OUT_SUB)
        x = x_ref[0, rows, :]
        u = (_rms(x, gpre_ref[...]) * (1.0 + mod(4)) + mod(3)).astype(BF16)
        yield
        z = _dot(u, wzg_ref[:, 0:SSD_INNER])
        a_mla = _dot(omla_ref[0, rows, :], wmo_ref[...])
        g0 = _dot(u, wzg_ref[:, gate_cols(0)])
        yield
        a_gqa = _dot(ogqa_ref[0, rows, :], wgo_ref[...])
        g1 = _dot(u, wzg_ref[:, gate_cols(1)])
        y = _rms(y_ref[0, rows, :] * (z * jax.nn.sigmoid(z)), gssd_ref[...]).astype(BF16)
        yield
        g2 = _dot(u, wzg_ref[:, gate_cols(2)])
        a_ssd = _dot(y, wso_ref[...])
        merged = jax.nn.sigmoid(g0) * a_mla + jax.nn.sigmoid(g1) * a_gqa
        yield
        merged = (merged + jax.nn.sigmoid(g2) * a_ssd).astype(BF16)
        out = _dot(merged, wout_ref[...])
        yield
        o_ref[0, rows, :] = x + mod(5) * _rms(out, gpost_ref[...])

    _emit_skewed([strand(r0) for r0 in range(0, tm, OUT_SUB)])


def _mixer_out(xc, modx, modc, omla, ogqa, y, lw, *, n_lat, n_rows):
    b, t_all, d = xc.shape
    tm = OUT_SUB
    row = lambda w: pl.BlockSpec((1, tm, w), lambda i, j: (i, j, 0))
    res = [lw["wzg"], lw["g_ssd"], lw["w_mla_o"], lw["w_gqa_o"], lw["w_ssd_o"], lw["w_out"]]
    return pl.pallas_call(
        functools.partial(_out_kernel, tm=tm, n_lat=n_lat),
        grid=(b, n_rows // tm),
        in_specs=[row(d), pl.BlockSpec((1, N_MOD, d), lambda i, j: (i, 0, 0)), _resident((N_MOD, d)),
                  _resident((1, d)), _resident((1, d)),
                  row(MLA_HEADS * MLA_V), row(GQA_HEADS * GQA_HD), row(SSD_INNER)]
        + [_resident(w.shape) for w in res],
        out_specs=row(d),
        out_shape=jax.ShapeDtypeStruct((b, t_all, d), F32),
        input_output_aliases={0: 0},
        compiler_params=_cparams(2),
        name="mixer_out",
    )(xc, modx, modc, lw["g_pre1"], lw["g_post1"], omla, ogqa, y, *res)


def _rope_tables(n_lat, n_ctx):
    rows = n_lat // GRID_W
    r = jnp.repeat(jnp.arange(rows, dtype=F32), GRID_W)
    c = jnp.tile(jnp.arange(GRID_W, dtype=F32), rows)

    def angles(rot_dim):
        n_freq = rot_dim // 4
        inv = ROPE_THETA ** (-jnp.arange(n_freq, dtype=F32) / n_freq)
        return jnp.concatenate([r[:, None] * inv, c[:, None] * inv], axis=-1)

    def finish(cos_l, sin_l):
        ident_c = jnp.ones((n_ctx, LANE), F32)
        ident_s = jnp.zeros((n_ctx, LANE), F32)
        return jnp.concatenate([cos_l, ident_c], 0), jnp.concatenate([sin_l, ident_s], 0)

    ang = angles(MLA_ROPE)
    cos, sin = jnp.cos(ang), jnp.sin(ang)
    ones = jnp.ones((n_lat, MLA_NOPE), F32)
    zeros = jnp.zeros((n_lat, MLA_NOPE), F32)
    tail1 = jnp.ones((n_lat, HEAD_PAD - MLA_NOPE - MLA_ROPE), F32)
    tail0 = jnp.zeros((n_lat, HEAD_PAD - MLA_NOPE - MLA_ROPE), F32)
    mla = finish(jnp.concatenate([ones, cos, cos, tail1], 1), jnp.concatenate([zeros, -sin, sin, tail0], 1))

    ang = angles(GQA_HD)
    cos, sin = jnp.cos(ang), jnp.sin(ang)
    gqa = finish(jnp.concatenate([cos, cos, cos, cos], 1), jnp.concatenate([-sin, sin, -sin, sin], 1))
    return mla + gqa


def _head_indicator():
    head = jnp.arange(GQA_HEADS * GQA_HD) // GQA_HD
    e = (head[:, None] == jnp.arange(LANE)[None, :])
    return (e.astype(F32) / GQA_HD).astype(BF16)


def _with_swapped_rope(w_rope):
    half = MLA_ROPE // 2
    return jnp.concatenate([w_rope, w_rope[..., half:], w_rope[..., :half]], axis=-1)


def _group_lanes(v):
    return v.reshape(2, SSD_GROUPS, SSD_HPG).transpose(1, 0, 2).reshape(SSD_GROUPS, 2 * SSD_HPG)


def _layer_weights(l, p):
    d = p["w_in"].shape[1]
    w = p["w_in"][l]
    kg = w[:, _C_KG:_C_VG].reshape(d, GQA_KV_HEADS, GQA_HD)
    kdup = jnp.concatenate([kg, kg], axis=-1).reshape(d, 2 * GQA_KV_HEADS * GQA_HD)
    kr = jnp.concatenate([jnp.zeros((d, MLA_NOPE), F32), _with_swapped_rope(w[:, _C_KROPE:_C_KG])], axis=-1)
    wdt = w[:, _C_DT:_C_QLAT].reshape(d, 2, SSD_GROUPS, SSD_HPG).transpose(0, 2, 1, 3).reshape(d, 2 * SSD_HEADS)
    wdt = jnp.pad(wdt, ((0, 0), (0, LANE - 2 * SSD_HEADS)))
    wa = jnp.concatenate([w[:, _C_KVLAT:_C_KROPE], kdup, w[:, _C_XBC:_C_DT], w[:, _C_QLAT:_C_QG],
                          w[:, _C_QG:_C_Z], kr, wdt], axis=1).astype(BF16)

    wkv = p["w_mla_kv_up"][l].reshape(MLA_KV_RANK, MLA_HEADS, MLA_NOPE + MLA_V)
    wk = jnp.pad(wkv[:, :, :MLA_NOPE], ((0, 0), (0, 0), (0, HEAD_PAD - MLA_NOPE)))
    wk = wk.reshape(MLA_KV_RANK, MLA_HEADS * HEAD_PAD).astype(BF16)
    wv_t = wkv[:, :, MLA_NOPE:].reshape(MLA_KV_RANK, MLA_HEADS * MLA_V).T.astype(BF16)
    wq = p["w_mla_q_up"][l].reshape(MLA_Q_RANK, MLA_HEADS, MLA_NOPE + MLA_ROPE)
    wq = jnp.concatenate([wq[:, :, :MLA_NOPE], _with_swapped_rope(wq[:, :, MLA_NOPE:])], axis=-1)
    wq = wq.reshape(MLA_Q_RANK, MLA_HEADS * HEAD_PAD).astype(BF16)

    par = jnp.zeros((SSD_GROUPS, 8, LANE), F32)
    par = par.at[:, 0, :2 * SSD_HPG].set(_group_lanes(p["dt_bias"][l]))
    par = par.at[:, 1, :2 * SSD_HPG].set(_group_lanes(p["a_log"][l]))
    skip = jnp.repeat(p["ssd_skip"][l].reshape(2, SSD_GROUPS, SSD_HPG), SSD_HD, axis=2).transpose(1, 0, 2)

    row = lambda v: v.reshape(1, -1).astype(F32)
    ffn = lambda s: (p["w_ffn_gate"][l, s].astype(BF16), p["w_ffn_up"][l, s].astype(BF16),
                     p["w_ffn_down"][l, s].astype(BF16))
    return {
        "ffn0": ffn(0), "ffn1": ffn(1),
        "g_pre0": row(p["g_pre"][l, 0]), "g_pre1": row(p["g_pre"][l, 1]), "g_pre2": row(p["g_pre"][l, 2]),
        "g_post0": row(p["g_post"][l, 0]), "g_post1": row(p["g_post"][l, 1]), "g_post2": row(p["g_post"][l, 2]),
        "wa": wa, "wvg_t": w[:, _C_VG:_C_XBC].T.astype(BF16),
        "g_mla_kv": row(p["g_mla_kv"][l]), "wk": wk, "wv_t": wv_t,
        "g_mla_q": row(p["g_mla_q"][l]), "wq": wq, "e": _head_indicator(),
        "g_gqa_k": row(jnp.tile(p["g_gqa_k"][l], 2 * GQA_KV_HEADS)),
        "g_gqa_q": row(jnp.tile(p["g_gqa_q"][l], GQA_HEADS)),
        "conv_w": p["conv_w"][l], "conv_b": row(p["conv_b"][l]), "ssd_par": par, "ssd_skip": skip,
        "wzg": w[:, _C_Z:].astype(BF16), "g_ssd": row(p["g_ssd"][l]),
        "w_mla_o": p["w_mla_o"][l].astype(BF16), "w_gqa_o": p["w_gqa_o"][l].astype(BF16),
        "w_ssd_o": p["w_ssd_o"][l].astype(BF16), "w_out": p["w_out"][l].astype(BF16),
    }


def kernel(x, c, ctx, c_ctx, w_mod, b_mod, g_pre, g_post, w_ffn_gate, w_ffn_up, w_ffn_down, w_in, g_mla_q, w_mla_q_up, g_mla_kv, w_mla_kv_up, g_gqa_q, g_gqa_k, conv_w, conv_b, dt_bias, a_log, ssd_skip, g_ssd, w_mla_o, w_gqa_o, w_ssd_o, w_out):
    b, n_lat, d = x.shape
    n_ctx = ctx.shape[1]
    t_all = n_lat + n_ctx
    depth = w_in.shape[0]
    assert n_lat % TOKEN_TILE == 0 and n_ctx % TOKEN_TILE == 0 and n_lat % GRID_W == 0
    assert w_in.shape[2] == _C_GATE + 3 * d and d == SSD_INNER
    p = dict(g_pre=g_pre, g_post=g_post, w_ffn_gate=w_ffn_gate, w_ffn_up=w_ffn_up, w_ffn_down=w_ffn_down,
             w_in=w_in, g_mla_q=g_mla_q, w_mla_q_up=w_mla_q_up, g_mla_kv=g_mla_kv, w_mla_kv_up=w_mla_kv_up,
             g_gqa_q=g_gqa_q, g_gqa_k=g_gqa_k, conv_w=conv_w, conv_b=conv_b, dt_bias=dt_bias, a_log=a_log,
             ssd_skip=ssd_skip, g_ssd=g_ssd, w_mla_o=w_mla_o, w_gqa_o=w_gqa_o, w_ssd_o=w_ssd_o, w_out=w_out)

    rows = -(-(b + 1) // 8) * 8
    c_all = jnp.concatenate([c, c_ctx[None, :], jnp.zeros((rows - b - 1, d), F32)], axis=0)
    mod = _modulation(c_all, w_mod, b_mod).reshape(depth, rows, N_MOD, d)
    tabs = _rope_tables(n_lat, n_ctx)
    xc = jnp.concatenate([x, ctx], axis=1)

    for l in range(depth):
        last = l == depth - 1
        lw = _layer_weights(l, p)
        modx, modc = mod[l, :b], mod[l, b]
        xc = _ffn(xc, modx, modc, lw["g_pre0"], lw["g_post0"], *lw["ffn0"],
                  k0=0, n_lat=n_lat, n_rows=t_all, alias=True)
        kmla, vtmla, qmla, kgqa, vtgqa, qgqa, xbc, dtr = _in_proj(xc, modx, modc, lw, tabs, n_lat=n_lat)
        y = _ssd(xbc, dtr, lw, n_lat=n_lat, n_ctx=n_ctx)
        omla = _attention("mla", qmla, kmla, vtmla, n_lat=n_lat, n_ctx=n_ctx, with_ctx_queries=not last)
        ogqa = _attention("gqa", qgqa, kgqa, vtgqa, n_lat=n_lat, n_ctx=n_ctx, with_ctx_queries=not last)
        n_rows = n_lat if last else t_all
        xc = _mixer_out(xc, modx, modc, omla, ogqa, y, lw, n_lat=n_lat, n_rows=n_rows)
        xc = _ffn(xc, modx, modc, lw["g_pre2"], lw["g_post2"], *lw["ffn1"],
                  k0=6, n_lat=n_lat, n_rows=n_rows, alias=not last)
    return xc
```
